```python
import jax, jax.numpy as jnp
from jax import lax
import numpy as np

D_MODEL = 2048
BATCH = 16
SEQ = 2048
DEPTH = 1
DEC_BATCH = 32
DEC_SEQ = 4
PAST_LEN = 16384
PAGE_SIZE = 128

HEAD_DIM = 128
HEADS_PER_GROUP = 4
GROUPS = ((128, 1), (512, 4), (2048, 16))
N_GROUPS = 3
N_ATT_HEADS = N_GROUPS * HEADS_PER_GROUP
ATT_WIDTH = N_ATT_HEADS * HEAD_DIM
ATT_OUT = HEADS_PER_GROUP * HEAD_DIM
BAND = 128
ATT_SCALE = HEAD_DIM ** -0.5
Q_BLOCK = 128
D_CONV = D_MODEL
CONV_W = 3
D_FF = 5632
LN_EPS = 1e-5
ALPHA = (2.0 * DEPTH) ** 0.25
BETA = (8.0 * DEPTH) ** -0.25

kernel_name = 'hybrid_shortconv_dilated_swa_convffn_step'


def _proj_sizes():
    return (D_CONV, D_CONV, D_CONV, ATT_WIDTH, ATT_WIDTH, ATT_WIDTH, D_MODEL, D_MODEL)


def _split_points():
    pts, acc = [], 0
    for s in _proj_sizes()[:-1]:
        acc += s
        pts.append(acc)
    return pts


def _layernorm(x, g, b):
    xf = x.astype(jnp.float32)
    mu = jnp.mean(xf, axis=-1, keepdims=True)
    var = jnp.mean(jnp.square(xf - mu), axis=-1, keepdims=True)
    return ((xf - mu) * lax.rsqrt(var + LN_EPS) * g.astype(jnp.float32) + b.astype(jnp.float32)).astype(x.dtype)


def _causal_dwconv(u, prev, w):
    t = u.shape[1]
    up = jnp.concatenate([prev.astype(u.dtype), u], axis=1)
    y = sum(w[i] * up[:, i:i + t] for i in range(CONV_W))
    return y.astype(u.dtype), up[:, t:]


def _alibi_slopes():
    h = jnp.arange(1, N_ATT_HEADS + 1, dtype=jnp.float32)
    return jnp.exp2(-8.0 * h / N_ATT_HEADS).reshape(N_GROUPS, HEADS_PER_GROUP)


def _dilated_band_prompt(q, k, v, slopes, dil):
    n, s, h, dh = q.shape
    u = s // dil
    nblk = -(-u // Q_BLOCK)
    upad = nblk * Q_BLOCK

    def classes(a, front):
        a = a.reshape(n, u, dil, h, dh).transpose(0, 2, 1, 3, 4)
        return jnp.pad(a, ((0, 0), (0, 0), (front, upad - u), (0, 0), (0, 0)))

    qb = classes(q, 0).reshape(n, dil, nblk, Q_BLOCK, h, dh)

    def windows(a):
        a = classes(a, Q_BLOCK).reshape(n, dil, nblk + 1, Q_BLOCK, h, dh)
        return jnp.concatenate([a[:, :, :-1], a[:, :, 1:]], axis=3)

    kw, vw = windows(k), windows(v)
    qi = jnp.arange(Q_BLOCK)[:, None]
    kj = jnp.arange(2 * Q_BLOCK)[None, :]
    du = Q_BLOCK + qi - kj
    uk = jnp.arange(nblk)[:, None, None] * Q_BLOCK + kj[None] - Q_BLOCK
    valid = (du >= 0) & (du <= BAND) & (uk >= 0)
    bias = slopes[:, None, None] * (dil * du).astype(jnp.float32)[None]
    sc = jnp.einsum('nrbqhd,nrbkhd->nrbhqk', qb, kw).astype(jnp.float32) * ATT_SCALE - bias
    sc = jnp.where(valid[None, None, :, None], sc, -jnp.inf)
    m = jnp.max(sc, axis=-1, keepdims=True)
    p = jnp.exp(sc - m)
    den = jnp.sum(p, axis=-1, keepdims=True)
    o = jnp.einsum('nrbhqk,nrbkhd->nrbqhd', p / den, vw.astype(jnp.float32))
    lse = jnp.swapaxes((m + jnp.log(den))[..., 0], 3, 4)

    def back(a):
        a = a.reshape((n, dil, upad) + a.shape[4:])[:, :, :u]
        a = jnp.swapaxes(a, 1, 2)
        return a.reshape((n, s) + a.shape[3:])

    return back(o), back(lse)


def _dilated_band_sample(q, k_all, v_all, slopes, dil):
    t = q.shape[1]
    l = k_all.shape[1] - t
    j = jnp.arange(BAND + 1)
    idx = l + jnp.arange(t)[:, None] - dil * j[None, :]
    valid = idx >= 0
    idx = jnp.maximum(idx, 0)
    kg = k_all[:, idx]
    vg = v_all[:, idx]
    sc = jnp.einsum('nthd,ntjhd->nhtj', q, kg).astype(jnp.float32) * ATT_SCALE \
        - slopes[:, None, None] * (dil * j).astype(jnp.float32)
    sc = jnp.where(valid, sc, -jnp.inf)
    m = jnp.max(sc, axis=-1, keepdims=True)
    p = jnp.exp(sc - m)
    den = jnp.sum(p, axis=-1, keepdims=True)
    o = jnp.einsum('nhtj,ntjhd->nthd', p / den, vg.astype(jnp.float32))
    lse = jnp.swapaxes((m + jnp.log(den))[..., 0], 1, 2)
    return o, lse


def _layer(x, conv_mix_prev, kv_prev, conv_ffn_prev, w_in, w_conv_mix, w_branch_a, w_branch_b, w_out,
           ln1_g, ln1_b, w_up, w_conv_ffn, w_down, ln2_g, ln2_b):
    n, t, _ = x.shape
    z = jnp.einsum('ntd,de->nte', x, w_in)
    h, gate_b, gate_c, q, k, v, g_a, g_b = jnp.split(z, _split_points(), axis=-1)

    ya, conv_mix_new = _causal_dwconv(gate_c * h, conv_mix_prev, w_conv_mix)
    ya = gate_b * ya

    shp = (n, t, N_GROUPS, HEADS_PER_GROUP, HEAD_DIM)
    q, k, v = q.reshape(shp), k.reshape(shp), v.reshape(shp)
    slopes = _alibi_slopes()
    outs, lses, kv_new = [], [], []
    for g, (win, dil) in enumerate(GROUPS):
        kg, vg = k[:, :, g], v[:, :, g]
        if kv_prev is None:
            o, l = _dilated_band_prompt(q[:, :, g], kg, vg, slopes[g], dil)
        else:
            kg = jnp.concatenate([kv_prev[g][:, :, 0].astype(x.dtype), kg], axis=1)
            vg = jnp.concatenate([kv_prev[g][:, :, 1].astype(x.dtype), vg], axis=1)
            o, l = _dilated_band_sample(q[:, :, g], kg, vg, slopes[g], dil)
        keep = min(win, kg.shape[1])
        kv_new.append(jnp.stack([kg[:, kg.shape[1] - keep:], vg[:, vg.shape[1] - keep:]], axis=2))
        outs.append(o)
        lses.append(l)
    wg = jax.nn.softmax(jnp.stack(lses), axis=0)
    yb = jnp.einsum('gnth,gnthd->nthd', wg, jnp.stack(outs)).reshape(n, t, ATT_OUT).astype(x.dtype)

    mixed = jax.nn.sigmoid(g_a) * (ya @ w_branch_a) + jax.nn.sigmoid(g_b) * (yb @ w_branch_b)
    x1 = _layernorm(ALPHA * x + mixed @ w_out, ln1_g, ln1_b)

    a, b = jnp.split(x1 @ w_up, 2, axis=-1)
    a, conv_ffn_new = _causal_dwconv(a, conv_ffn_prev, w_conv_ffn)
    y = _layernorm(ALPHA * x1 + (jax.nn.silu(a) * b) @ w_down, ln2_g, ln2_b)
    return y, conv_mix_new, kv_new, conv_ffn_new


def setup_inputs(seed: int = 0) -> dict:
    key = jax.random.key(seed)
    ks = jax.random.split(key, 24)
    f32 = jnp.float32

    def nrm(k, shape, scale):
        return jax.random.normal(k, shape, f32) * scale

    col_scale = jnp.concatenate([jnp.full((sz,), sc, f32) for sz, sc in
                                 zip(_proj_sizes(), (BETA, 1.0, 1.0, 1.0, 1.0, BETA, 1.0, 1.0))])
    w_in = nrm(ks[0], (D_MODEL, sum(_proj_sizes())), D_MODEL ** -0.5) * col_scale
    kv_shape = lambda win: (DEC_BATCH, min(win, PAST_LEN), 2, HEADS_PER_GROUP, HEAD_DIM)
    return {
        'x_prompt': nrm(ks[1], (BATCH, SEQ, D_MODEL), 1.0),
        'x_sample': nrm(ks[2], (DEC_BATCH, DEC_SEQ, D_MODEL), 1.0),
        'state_conv_mix': nrm(ks[3], (DEC_BATCH, CONV_W - 1, D_CONV), 1.0),
        'cache_kv0': nrm(ks[4], kv_shape(GROUPS[0][0]), 1.0),
        'cache_kv1': nrm(ks[5], kv_shape(GROUPS[1][0]), 1.0),
        'cache_kv2': nrm(ks[6], kv_shape(GROUPS[2][0]), 1.0),
        'state_conv_ffn': nrm(ks[7], (DEC_BATCH, CONV_W - 1, D_FF), 1.0),
        'w_in': w_in,
        'w_conv_mix': nrm(ks[8], (CONV_W, D_CONV), CONV_W ** -0.5),
        'w_branch_a': nrm(ks[9], (D_CONV, D_MODEL), BETA * D_CONV ** -0.5),
        'w_branch_b': nrm(ks[10], (ATT_OUT, D_MODEL), BETA * ATT_OUT ** -0.5),
        'w_out': nrm(ks[11], (D_MODEL, D_MODEL), BETA * D_MODEL ** -0.5),
        'ln1_g': 1.0 + nrm(ks[12], (D_MODEL,), 0.02),
        'ln1_b': nrm(ks[13], (D_MODEL,), 0.02),
        'w_up': nrm(ks[14], (D_MODEL, 2 * D_FF), BETA * D_MODEL ** -0.5),
        'w_conv_ffn': nrm(ks[15], (CONV_W, D_FF), CONV_W ** -0.5),
        'w_down': nrm(ks[16], (D_FF, D_MODEL), BETA * D_FF ** -0.5),
        'ln2_g': 1.0 + nrm(ks[17], (D_MODEL,), 0.02),
        'ln2_b': nrm(ks[18], (D_MODEL,), 0.02),
    }


def reference(x_prompt, x_sample, state_conv_mix, cache_kv0, cache_kv1, cache_kv2, state_conv_ffn,
              w_in, w_conv_mix, w_branch_a, w_branch_b, w_out, ln1_g, ln1_b,
              w_up, w_conv_ffn, w_down, ln2_g, ln2_b):
    weights = (w_in, w_conv_mix, w_branch_a, w_branch_b, w_out, ln1_g, ln1_b,
               w_up, w_conv_ffn, w_down, ln2_g, ln2_b)
    nb = x_prompt.shape[0]
    zeros_mix = jnp.zeros((nb, CONV_W - 1, D_CONV), x_prompt.dtype)
    zeros_ffn = jnp.zeros((nb, CONV_W - 1, D_FF), x_prompt.dtype)
    y_prompt, cm_p, kv_p, cf_p = _layer(x_prompt, zeros_mix, None, zeros_ffn, *weights)
    y_sample, cm_s, kv_s, cf_s = _layer(x_sample, state_conv_mix, (cache_kv0, cache_kv1, cache_kv2),
                                        state_conv_ffn, *weights)
    return (y_prompt, y_sample, cm_p, kv_p[0], kv_p[1], kv_p[2], cf_p,
            cm_s, kv_s[0], kv_s[1], kv_s[2], cf_s)
```

```python
import functools

import jax
import jax.numpy as jnp
from jax import lax
from jax.experimental import pallas as pl
from jax.experimental.pallas import tpu as pltpu

F32 = jnp.float32
BF16 = jnp.bfloat16

D_MODEL = 2048
HEAD_DIM = 128
HEADS = 4
GROUPS = ((128, 1), (512, 4), (2048, 16))
N_GROUPS = 3
GROUP_W = HEADS * HEAD_DIM
ATT_W = N_GROUPS * GROUP_W
BAND = 128
ATT_SCALE = HEAD_DIM ** -0.5
D_FF = 5632
LN_EPS = 1e-5
ALPHA = 2.0 ** 0.25

SUBLANES = 8
V7X_VMEM_LIMIT = 58 * 1024 * 1024


def _cparams(n_axes, vmem=V7X_VMEM_LIMIT):
    return pltpu.CompilerParams(dimension_semantics=("arbitrary",) * n_axes, vmem_limit_bytes=vmem)


def _dot(a, b):
    return jnp.dot(a, b, preferred_element_type=F32)


def _layernorm(z, g, b):
    mu = jnp.mean(z, axis=-1, keepdims=True)
    zc = z - mu
    var = jnp.mean(zc * zc, axis=-1, keepdims=True)
    return zc * lax.rsqrt(var + LN_EPS) * g + b


def _conv_taps(buf, u, prev_rows, w_ref, tm, shift):
    base = -(-2 * shift // SUBLANES) * SUBLANES
    buf[base:base + tm, :] = u
    w = w_ref[...]
    y = w[0:1, :] * buf[base - 2 * shift:base - 2 * shift + tm, :]
    y = y + w[1:2, :] * buf[base - shift:base - shift + tm, :]
    return y + w[2:3, :] * u


def _load_prev(buf, prev_ref, carry_ref, i, j, tps, shift):
    p = 2 * shift
    base = -(-p // SUBLANES) * SUBLANES
    if prev_ref is not None:
        buf[base - p:base, :] = prev_ref[...]
    else:
        @pl.when(i % tps == 0)
        def _():
            buf[base - p:base, :] = jnp.zeros((p, buf.shape[1]), F32)

        @pl.when(i % tps != 0)
        def _():
            buf[base - p:base, :] = carry_ref[j]


def _proj_mix_kernel(*refs, tm, shift, tps, has_prev):
    if has_prev:
        x_ref, wh_ref, wb_ref, wc_ref, wconv_ref, prev_ref, xb_ref, ya_ref, cm_ref, ubuf = refs
        carry_ref = None
    else:
        x_ref, wh_ref, wb_ref, wc_ref, wconv_ref, xb_ref, ya_ref, cm_ref, ubuf, carry_ref = refs
        prev_ref = None
    i, j = pl.program_id(0), pl.program_id(1)

    @pl.when(j == 0)
    def _():
        xb_ref[...] = x_ref[...].astype(BF16)

    xb = xb_ref[...]
    h = _dot(xb, wh_ref[...])
    gate_c = _dot(xb, wc_ref[...])
    u = gate_c * h
    _load_prev(ubuf, prev_ref, carry_ref, i, j, tps, shift)
    y = _conv_taps(ubuf, u, None, wconv_ref, tm, shift)
    gate_b = _dot(xb, wb_ref[...])
    ya_ref[...] = (gate_b * y).astype(BF16)
    base = -(-2 * shift // SUBLANES) * SUBLANES
    last = ubuf[base + tm - 2 * shift:base + tm, :]
    cm_ref[0] = last
    if carry_ref is not None:
        carry_ref[j] = last


def _proj_mix(x2d, w_hbc, w_conv, prev, *, tm, tn, shift, tps):
    m = x2d.shape[0]
    c = w_conv.shape[1]
    nj = c // tn
    has_prev = prev is not None
    p = 2 * shift
    base = -(-p // SUBLANES) * SUBLANES
    in_specs = [
        pl.BlockSpec((tm, D_MODEL), lambda i, j: (i, 0)),
        pl.BlockSpec((D_MODEL, tn), lambda i, j: (0, j)),
        pl.BlockSpec((D_MODEL, tn), lambda i, j: (0, j + nj)),
        pl.BlockSpec((D_MODEL, tn), lambda i, j: (0, j + 2 * nj)),
        pl.BlockSpec((3, tn), lambda i, j: (0, j)),
    ]
    args = [x2d, w_hbc, w_hbc, w_hbc, w_conv]
    scratch = [pltpu.VMEM((base + tm, tn), F32)]
    if has_prev:
        in_specs.append(pl.BlockSpec((p, tn), lambda i, j: (0, j)))
        args.append(prev)
    else:
        scratch.append(pltpu.VMEM((nj, p, tn), F32))
    return pl.pallas_call(
        functools.partial(_proj_mix_kernel, tm=tm, shift=shift, tps=tps, has_prev=has_prev),
        grid=(m // tm, nj),
        in_specs=in_specs,
        out_specs=[
            pl.BlockSpec((tm, D_MODEL), lambda i, j: (i, 0)),
            pl.BlockSpec((tm, tn), lambda i, j: (i, j)),
            pl.BlockSpec((1, p, tn), lambda i, j: (i, 0, j)),
        ],
        out_shape=[
            jax.ShapeDtypeStruct((m, D_MODEL), BF16),
            jax.ShapeDtypeStruct((m, c), BF16),
            jax.ShapeDtypeStruct((m // tm, p, c), F32),
        ],
        scratch_shapes=scratch,
        compiler_params=_cparams(2),
        name="proj_mix",
    )(*args)


def _proj_qkv_kernel(xb_ref, w_ref, q_ref, k_ref, v_ref, kvf_ref):
    r = _dot(xb_ref[...], w_ref[0])
    q_ref[0] = r[:, :GROUP_W].astype(BF16)
    k_ref[0] = r[:, GROUP_W:2 * GROUP_W].astype(BF16)
    v_ref[0] = r[:, 2 * GROUP_W:].astype(BF16)
    kvf_ref[0] = r[:, GROUP_W:]


def _proj_qkv(xb, w_qkv, *, tm):
    m = xb.shape[0]
    blk = lambda w: pl.BlockSpec((1, tm, w), lambda g, i: (g, i, 0))
    return pl.pallas_call(
        _proj_qkv_kernel,
        grid=(N_GROUPS, m // tm),
        in_specs=[
            pl.BlockSpec((tm, D_MODEL), lambda g, i: (i, 0)),
            pl.BlockSpec((1, D_MODEL, 3 * GROUP_W), lambda g, i: (g, 0, 0)),
        ],
        out_specs=[blk(GROUP_W), blk(GROUP_W), blk(GROUP_W), blk(2 * GROUP_W)],
        out_shape=[
            jax.ShapeDtypeStruct((N_GROUPS, m, GROUP_W), BF16),
            jax.ShapeDtypeStruct((N_GROUPS, m, GROUP_W), BF16),
            jax.ShapeDtypeStruct((N_GROUPS, m, GROUP_W), BF16),
            jax.ShapeDtypeStruct((N_GROUPS, m, 2 * GROUP_W), F32),
        ],
        compiler_params=_cparams(2),
        name="proj_qkv",
    )(xb, w_qkv)


def _proj_gate_kernel(xb_ref, w_ref, o_ref):
    z = _dot(xb_ref[...], w_ref[...])
    o_ref[...] = (1.0 / (1.0 + jnp.exp(-z))).astype(BF16)


def _proj_gate(xb, w_gate, *, tm, tn):
    m = xb.shape[0]
    n = w_gate.shape[1]
    return pl.pallas_call(
        _proj_gate_kernel,
        grid=(n // tn, m // tm),
        in_specs=[
            pl.BlockSpec((tm, D_MODEL), lambda c, i: (i, 0)),
            pl.BlockSpec((D_MODEL, tn), lambda c, i: (0, c)),
        ],
        out_specs=pl.BlockSpec((tm, tn), lambda c, i: (i, c)),
        out_shape=jax.ShapeDtypeStruct((m, n), BF16),
        compiler_params=_cparams(2),
        name="proj_gate",
    )(xb, w_gate)


def _attn_kernel(q_ref, k_ref, v_ref, bias_ref, o_ref, lse_ref, *, dil, n_u):
    nblk = n_u // BAND
    lane = lax.broadcasted_iota(jnp.int32, (BAND, HEAD_DIM), 1)
    for r in range(dil):
        for qb in range(nblk):
            rows = slice(qb * BAND, (qb + 1) * BAND)
            krows = rows if qb == 0 else slice((qb - 1) * BAND, (qb + 1) * BAND)
            lse_tile = jnp.zeros((BAND, HEAD_DIM), F32)
            for h in range(HEADS):
                cols = slice((r * HEADS + h) * HEAD_DIM, (r * HEADS + h + 1) * HEAD_DIM)
                q = q_ref[0, rows, cols]
                kw = k_ref[0, krows, cols]
                vw = v_ref[0, krows, cols]
                bias = bias_ref[h, :, BAND:] if qb == 0 else bias_ref[h]
                s = lax.dot_general(q, kw, (((1,), (1,)), ((), ())), preferred_element_type=F32)
                s = s * ATT_SCALE + bias
                mx = jnp.max(s, axis=-1, keepdims=True)
                p = jnp.exp(s - mx)
                den = jnp.sum(p, axis=-1, keepdims=True)
                o_ref[0, rows, cols] = _dot(p.astype(BF16), vw) / den
                lse_tile = jnp.where(lane == h, mx + jnp.log(den), lse_tile)
            lse_ref[0, rows, r * HEAD_DIM:(r + 1) * HEAD_DIM] = lse_tile


def _attention(q, k, v, bias, *, nseq, seqlen, dil):
    n_u = seqlen // dil
    view = lambda a: a.reshape(nseq, n_u, dil * GROUP_W)
    spec = pl.BlockSpec((1, n_u, dil * GROUP_W), lambda b: (b, 0, 0))
    o, lse = pl.pallas_call(
        functools.partial(_attn_kernel, dil=dil, n_u=n_u),
        grid=(nseq,),
        in_specs=[spec, spec, spec, pl.BlockSpec((HEADS, BAND, 2 * BAND), lambda b: (0, 0, 0))],
        out_specs=[spec, pl.BlockSpec((1, n_u, dil * HEAD_DIM), lambda b: (b, 0, 0))],
        out_shape=[
            jax.ShapeDtypeStruct((nseq, n_u, dil * GROUP_W), F32),
            jax.ShapeDtypeStruct((nseq, n_u, dil * HEAD_DIM), F32),
        ],
        compiler_params=_cparams(1),
        name=f"attn_dil{dil}",
    )(view(q), view(k), view(v), bias)
    return o.reshape(nseq * seqlen, GROUP_W), lse.reshape(nseq * seqlen, HEAD_DIM)


def _merge_kernel(o0_ref, o1_ref, o2_ref, l0_ref, l1_ref, l2_ref, yb_ref):
    o_refs = (o0_ref, o1_ref, o2_ref)
    l_refs = (l0_ref, l1_ref, l2_ref)
    for h in range(HEADS):
        cols = slice(h * HEAD_DIM, (h + 1) * HEAD_DIM)
        ls = [l[:, h:h + 1] for l in l_refs]
        mx = jnp.maximum(jnp.maximum(ls[0], ls[1]), ls[2])
        ws = [jnp.exp(l - mx) for l in ls]
        den = ws[0] + ws[1] + ws[2]
        acc = ws[0] * o_refs[0][:, cols] + ws[1] * o_refs[1][:, cols] + ws[2] * o_refs[2][:, cols]
        yb_ref[:, cols] = (acc / den).astype(BF16)


def _merge(os_, ls_, *, tm):
    m = os_[0].shape[0]
    ospec = pl.BlockSpec((tm, GROUP_W), lambda i: (i, 0))
    lspec = pl.BlockSpec((tm, HEAD_DIM), lambda i: (i, 0))
    return pl.pallas_call(
        _merge_kernel,
        grid=(m // tm,),
        in_specs=[ospec] * 3 + [lspec] * 3,
        out_specs=ospec,
        out_shape=jax.ShapeDtypeStruct((m, GROUP_W), BF16),
        compiler_params=_cparams(1),
        name="attn_merge",
    )(*os_, *ls_)


def _mix_kernel(ya_ref, yb_ref, ga_ref, gb_ref, x_ref, wa_ref, wb_ref, wo_ref, g_ref, b_ref,
                x1_ref, x1b_ref):
    n = pl.program_id(1)

    @pl.when(n == 0)
    def _():
        x1_ref[...] = ALPHA * x_ref[...]

    a = _dot(ya_ref[...], wa_ref[...])
    b = _dot(yb_ref[...], wb_ref[...])
    mixed = ga_ref[...].astype(F32) * a + gb_ref[...].astype(F32) * b
    x1_ref[...] += _dot(mixed.astype(BF16), wo_ref[...])

    @pl.when(n == pl.num_programs(1) - 1)
    def _():
        out = _layernorm(x1_ref[...], g_ref[...], b_ref[...])
        x1_ref[...] = out
        x1b_ref[...] = out.astype(BF16)


def _mix(ya, yb, sg, x2d, w_a, w_b, w_o, ln_g, ln_b, *, tm, tn):
    m = x2d.shape[0]
    nn = D_MODEL // tn
    row = lambda w: pl.BlockSpec((tm, w), lambda i, n: (i, 0))
    return pl.pallas_call(
        _mix_kernel,
        grid=(m // tm, nn),
        in_specs=[
            row(D_MODEL), row(GROUP_W),
            pl.BlockSpec((tm, tn), lambda i, n: (i, n)),
            pl.BlockSpec((tm, tn), lambda i, n: (i, n + nn)),
            row(D_MODEL),
            pl.BlockSpec((D_MODEL, tn), lambda i, n: (0, n)),
            pl.BlockSpec((GROUP_W, tn), lambda i, n: (0, n)),
            pl.BlockSpec((tn, D_MODEL), lambda i, n: (n, 0)),
            pl.BlockSpec((1, D_MODEL), lambda i, n: (0, 0)),
            pl.BlockSpec((1, D_MODEL), lambda i, n: (0, 0)),
        ],
        out_specs=[row(D_MODEL), row(D_MODEL)],
        out_shape=[jax.ShapeDtypeStruct((m, D_MODEL), F32), jax.ShapeDtypeStruct((m, D_MODEL), BF16)],
        compiler_params=_cparams(2),
        name="mix_out",
    )(ya, yb, sg, sg, x2d, w_a, w_b, w_o, ln_g, ln_b)


def _ffn_kernel(*refs, tm, shift, tps, has_prev):
    if has_prev:
        (x1b_ref, x1_ref, wa_ref, wb_ref, wd_ref, wconv_ref, g_ref, b_ref, prev_ref,
         y_ref, cf_ref, abuf) = refs
        carry_ref = None
    else:
        (x1b_ref, x1_ref, wa_ref, wb_ref, wd_ref, wconv_ref, g_ref, b_ref,
         y_ref, cf_ref, abuf, carry_ref) = refs
        prev_ref = None
    i, f = pl.program_id(0), pl.program_id(1)

    @pl.when(f == 0)
    def _():
        y_ref[...] = ALPHA * x1_ref[...]

    xb = x1b_ref[...]
    a = _dot(xb, wa_ref[...])
    _load_prev(abuf, prev_ref, carry_ref, i, f, tps, shift)
    ac = _conv_taps(abuf, a, None, wconv_ref, tm, shift)
    b = _dot(xb, wb_ref[...])
    hm = ac * (1.0 / (1.0 + jnp.exp(-ac))) * b
    y_ref[...] += _dot(hm.astype(BF16), wd_ref[...])
    base = -(-2 * shift // SUBLANES) * SUBLANES
    last = abuf[base + tm - 2 * shift:base + tm, :]
    cf_ref[0] = last
    if carry_ref is not None:
        carry_ref[f] = last

    @pl.when(f == pl.num_programs(1) - 1)
    def _():
        y_ref[...] = _layernorm(y_ref[...], g_ref[...], b_ref[...])


def _ffn(x1b, x1, w_up, w_down, w_conv, ln_g, ln_b, prev, *, tm, tf, shift, tps):
    m = x1.shape[0]
    nf = D_FF // tf
    has_prev = prev is not None
    p = 2 * shift
    base = -(-p // SUBLANES) * SUBLANES
    row = lambda: pl.BlockSpec((tm, D_MODEL), lambda i, f: (i, 0))
    in_specs = [
        row(), row(),
        pl.BlockSpec((D_MODEL, tf), lambda i, f: (0, f)),
        pl.BlockSpec((D_MODEL, tf), lambda i, f: (0, f + nf)),
        pl.BlockSpec((tf, D_MODEL), lambda i, f: (f, 0)),
        pl.BlockSpec((3, tf), lambda i, f: (0, f)),
        pl.BlockSpec((1, D_MODEL), lambda i, f: (0, 0)),
        pl.BlockSpec((1, D_MODEL), lambda i, f: (0, 0)),
    ]
    args = [x1b, x1, w_up, w_up, w_down, w_conv, ln_g, ln_b]
    scratch = [pltpu.VMEM((base + tm, tf), F32)]
    if has_prev:
        in_specs.append(pl.BlockSpec((p, tf), lambda i, f: (0, f)))
        args.append(prev)
    else:
        scratch.append(pltpu.VMEM((nf, p, tf), F32))
    return pl.pallas_call(
        functools.partial(_ffn_kernel, tm=tm, shift=shift, tps=tps, has_prev=has_prev),
        grid=(m // tm, nf),
        in_specs=in_specs,
        out_specs=[row(), pl.BlockSpec((1, p, tf), lambda i, f: (i, 0, f))],
        out_shape=[
            jax.ShapeDtypeStruct((m, D_MODEL), F32),
            jax.ShapeDtypeStruct((m // tm, p, D_FF), F32),
        ],
        scratch_shapes=scratch,
        compiler_params=_cparams(2),
        name="conv_ffn",
    )(*args)


def _sample_attn_kernel(slope_ref, q_ref, c0_ref, c1_ref, c2_ref, n0_ref, n1_ref, n2_ref, o_ref,
                        *, n_new):
    c_refs = (c0_ref, c1_ref, c2_ref)
    n_refs = (n0_ref, n1_ref, n2_ref)
    kv_w = 2 * GROUP_W
    m_idx = lax.broadcasted_iota(jnp.int32, (BAND, 1), 0)
    n_idx = lax.broadcasted_iota(jnp.int32, (n_new, 1), 0)
    for h in range(HEADS):
        hc = slice(h * HEAD_DIM, (h + 1) * HEAD_DIM)
        hv = slice(GROUP_W + h * HEAD_DIM, GROUP_W + (h + 1) * HEAD_DIM)
        for t in range(n_new):
            scores, values = [], []
            for g, (_, dil) in enumerate(GROUPS):
                slope = slope_ref[g * HEADS + h]
                qv = q_ref[0, t:t + 1, g * GROUP_W + h * HEAD_DIM:g * GROUP_W + (h + 1) * HEAD_DIM]
                qv = qv.astype(F32)
                off = 0 if dil == 1 else t * kv_w
                kc = c_refs[g][0, :, off + h * HEAD_DIM:off + (h + 1) * HEAD_DIM]
                vc = c_refs[g][0, :, off + GROUP_W + h * HEAD_DIM:off + GROUP_W + (h + 1) * HEAD_DIM]
                jc = (BAND + t - m_idx) if dil == 1 else (BAND - m_idx)
                sc = jnp.sum(kc * qv, axis=-1, keepdims=True) * ATT_SCALE - slope * (dil * jc).astype(F32)
                if dil == 1:
                    sc = jnp.where(jc <= BAND, sc, -jnp.inf)
                    kn, vn = n_refs[g][0, :, hc], n_refs[g][0, :, hv]
                    jn = t - n_idx
                    sn = jnp.sum(kn * qv, axis=-1, keepdims=True) * ATT_SCALE - slope * jn.astype(F32)
                    sn = jnp.where(jn >= 0, sn, -jnp.inf)
                else:
                    kn, vn = n_refs[g][0, t:t + 1, hc], n_refs[g][0, t:t + 1, hv]
                    sn = jnp.sum(kn * qv, axis=-1, keepdims=True) * ATT_SCALE
                scores += [sc, sn]
                values += [vc, vn]
            mx = functools.reduce(jnp.maximum, [jnp.max(s, axis=0, keepdims=True) for s in scores])
            ps = [jnp.exp(s - mx) for s in scores]
            den = functools.reduce(jnp.add, [jnp.sum(p, axis=0, keepdims=True) for p in ps])
            acc = functools.reduce(
                jnp.add, [jnp.sum(p * v, axis=0, keepdims=True) for p, v in zip(ps, values)])
            o_ref[0, t:t + 1, hc] = acc / den


def _sample_attention(slopes, q_b, caches, news):
    nb, n_new, _ = q_b.shape
    kv_w = 2 * GROUP_W
    c_args, c_specs = [], []
    for (win, dil), c in zip(GROUPS, caches):
        res = min(dil, n_new)
        c_args.append(c.reshape(nb, win // dil, dil * kv_w))
        c_specs.append(pl.BlockSpec((1, win // dil, res * kv_w), lambda b: (b, 0, 0)))
    nspec = pl.BlockSpec((1, n_new, kv_w), lambda b: (b, 0, 0))
    return pl.pallas_call(
        functools.partial(_sample_attn_kernel, n_new=n_new),
        grid=(nb,),
        in_specs=[pl.BlockSpec(memory_space=pltpu.SMEM),
                  pl.BlockSpec((1, n_new, ATT_W), lambda b: (b, 0, 0))] + c_specs + [nspec] * 3,
        out_specs=pl.BlockSpec((1, n_new, GROUP_W), lambda b: (b, 0, 0)),
        out_shape=jax.ShapeDtypeStruct((nb, n_new, GROUP_W), F32),
        compiler_params=_cparams(1),
        name="sample_attn",
    )(slopes, q_b, *c_args, *news)


def _shift_cache_kernel(c_ref, nxt_ref, new_ref, o_ref, *, rows, n_new):
    i = pl.program_id(1)
    last = pl.num_programs(1) - 1
    o_ref[0, 0:rows - n_new, :] = c_ref[0, n_new:rows, :]

    @pl.when(i < last)
    def _():
        o_ref[0, rows - n_new:rows, :] = nxt_ref[0, 0:n_new, :]

    @pl.when(i == last)
    def _():
        o_ref[0, rows - n_new:rows, :] = new_ref[0]


def _shift_cache(cache, new, *, rows):
    nb, length, width = cache.shape
    n_new = new.shape[1]
    nblk = length // rows
    per = rows // SUBLANES
    return pl.pallas_call(
        functools.partial(_shift_cache_kernel, rows=rows, n_new=n_new),
        grid=(nb, nblk),
        in_specs=[
            pl.BlockSpec((1, rows, width), lambda b, i: (b, i, 0)),
            pl.BlockSpec((1, SUBLANES, width), lambda b, i: (b, jnp.minimum(i + 1, nblk - 1) * per, 0)),
            pl.BlockSpec((1, n_new, width), lambda b, i: (b, 0, 0)),
        ],
        out_specs=pl.BlockSpec((1, rows, width), lambda b, i: (b, i, 0)),
        out_shape=jax.ShapeDtypeStruct(cache.shape, cache.dtype),
        compiler_params=_cparams(2),
        name="shift_cache",
    )(cache, cache, new)


def _alibi_slopes():
    h = jnp.arange(1, N_GROUPS * HEADS + 1, dtype=F32)
    return jnp.exp2(-8.0 * h / (N_GROUPS * HEADS))


def _prompt_bias(slopes, g, dil):
    qi = jnp.arange(BAND)[:, None]
    kj = jnp.arange(2 * BAND)[None, :]
    du = BAND + qi - kj
    valid = (du >= 0) & (du <= BAND)
    pen = slopes[g * HEADS:(g + 1) * HEADS, None, None] * (dil * du).astype(F32)[None]
    return jnp.where(valid[None], -pen, -jnp.inf)


def kernel(x_prompt, x_sample, state_conv_mix, cache_kv0, cache_kv1, cache_kv2, state_conv_ffn,
           w_in, w_conv_mix, w_branch_a, w_branch_b, w_out, ln1_g, ln1_b,
           w_up, w_conv_ffn, w_down, ln2_g, ln2_b):
    nb_p, seq, _ = x_prompt.shape
    nb_s, n_new, _ = x_sample.shape
    c0, c1, c2 = D_MODEL, 2 * D_MODEL, 3 * D_MODEL
    q0, k0, v0, g0 = c2, c2 + ATT_W, c2 + 2 * ATT_W, c2 + 3 * ATT_W

    w_hbc = w_in[:, :c2].astype(BF16)
    w_qkv = jnp.stack([
        jnp.concatenate([w_in[:, o + g * GROUP_W:o + (g + 1) * GROUP_W] for o in (q0, k0, v0)], axis=1)
        for g in range(N_GROUPS)]).astype(BF16)
    w_gate = w_in[:, g0:].astype(BF16)
    w_a, w_b, w_o = w_branch_a.astype(BF16), w_branch_b.astype(BF16), w_out.astype(BF16)
    w_upb, w_dnb = w_up.astype(BF16), w_down.astype(BF16)
    ln1 = (ln1_g.reshape(1, -1), ln1_b.reshape(1, -1))
    ln2 = (ln2_g.reshape(1, -1), ln2_b.reshape(1, -1))
    slopes = _alibi_slopes()

    tm = 1024
    tps = seq // tm
    x2d = x_prompt.reshape(nb_p * seq, D_MODEL)
    xb, ya, cm_p = _proj_mix(x2d, w_hbc, w_conv_mix, None, tm=tm, tn=256, shift=1, tps=tps)
    q, k, v, kvf = _proj_qkv(xb, w_qkv, tm=tm)
    sg = _proj_gate(xb, w_gate, tm=tm, tn=1024)
    os_, ls_ = [], []
    for g, (_, dil) in enumerate(GROUPS):
        o, l = _attention(q[g], k[g], v[g], _prompt_bias(slopes, g, dil), nseq=nb_p, seqlen=seq, dil=dil)
        os_.append(o)
        ls_.append(l)
    yb = _merge(os_, ls_, tm=tm)
    x1, x1b = _mix(ya, yb, sg, x2d, w_a, w_b, w_o, *ln1, tm=512, tn=512)
    y2d, cf_p = _ffn(x1b, x1, w_upb, w_dnb, w_conv_ffn, *ln2, None, tm=512, tf=512, shift=1, tps=seq // 512)
    y_prompt = y2d.reshape(nb_p, seq, D_MODEL)
    kv_p = []
    for g, (win, _) in enumerate(GROUPS):
        keep = min(win, seq)
        kv = kvf[g].reshape(nb_p, seq, 2, HEADS, HEAD_DIM)
        kv_p.append(kv[:, seq - keep:])

    ms = nb_s * n_new
    to_tm = lambda a: jnp.swapaxes(a, 0, 1).reshape(a.shape[0] * a.shape[1], a.shape[2])
    from_tm = lambda a, t: jnp.swapaxes(a.reshape(t, nb_s, a.shape[-1]), 0, 1)
    xs = to_tm(x_sample)
    xbs, yas, cm_s = _proj_mix(xs, w_hbc, w_conv_mix, to_tm(state_conv_mix),
                               tm=ms, tn=512, shift=nb_s, tps=1)
    qs, _, _, kvfs = _proj_qkv(xbs, w_qkv, tm=ms)
    sgs = _proj_gate(xbs, w_gate, tm=ms, tn=1024)
    q_b = jnp.concatenate([from_tm(qs[g], n_new) for g in range(N_GROUPS)], axis=-1)
    news = [from_tm(kvfs[g], n_new) for g in range(N_GROUPS)]
    caches = [c.reshape(nb_s, c.shape[1], 2 * GROUP_W) for c in (cache_kv0, cache_kv1, cache_kv2)]
    yb_s = _sample_attention(slopes, q_b, caches, news)
    ybs = to_tm(yb_s).astype(BF16)
    x1s, x1bs = _mix(yas, ybs, sgs, xs, w_a, w_b, w_o, *ln1, tm=ms, tn=512)
    ys, cf_s = _ffn(x1bs, x1s, w_upb, w_dnb, w_conv_ffn, *ln2, to_tm(state_conv_ffn),
                    tm=ms, tf=512, shift=nb_s, tps=1)
    y_sample = from_tm(ys, n_new)
    kv_s = [_shift_cache(c, n, rows=min(c.shape[1], 1024)).reshape(full.shape)
            for c, n, full in zip(caches, news, (cache_kv0, cache_kv1, cache_kv2))]

    cm_p = cm_p[tps - 1::tps]
    cf_p = cf_p[seq // 512 - 1::seq // 512]
    return (y_prompt, y_sample, cm_p, kv_p[0], kv_p[1], kv_p[2], cf_p,
            from_tm(cm_s[0], 2), kv_s[0], kv_s[1], kv_s[2], from_tm(cf_s[0], 2))
```

```python
import functools

import jax
import jax.numpy as jnp
from jax import lax
from jax.experimental import pallas as pl
from jax.experimental.pallas import tpu as pltpu

F32 = jnp.float32
BF16 = jnp.bfloat16

D_MODEL = 2048
HEAD_DIM = 128
HEADS = 4
GROUPS = ((128, 1), (512, 4), (2048, 16))
N_GROUPS = 3
GROUP_W = HEADS * HEAD_DIM
KV_ROWS = 2 * HEADS
BAND = 128
ATT_SCALE = HEAD_DIM ** -0.5
D_FF = 5632
LN_EPS = 1e-5
ALPHA = 2.0 ** 0.25

SUBLANES = 8
LANES = 128
COL_TILE = 512
DEINT_STEP = 4
V7X_VMEM_LIMIT = 58 * 1024 * 1024

_BLK_H, _BLK_BG, _BLK_CG = 0, 4, 8
_BLK_Q, _BLK_K, _BLK_V = 12, 15, 18
_BLK_GATE = 21


def _cparams(n_axes, vmem=V7X_VMEM_LIMIT):
    return pltpu.CompilerParams(dimension_semantics=("arbitrary",) * n_axes, vmem_limit_bytes=vmem)


def _dot(a, b):
    return jnp.dot(a, b, preferred_element_type=F32)


def _sigmoid(z):
    return 1.0 / (1.0 + jnp.exp(-z))


def _layernorm(z, g, b):
    mu = jnp.mean(z, axis=-1, keepdims=True)
    zc = z - mu
    var = jnp.mean(zc * zc, axis=-1, keepdims=True)
    return zc * lax.rsqrt(var + LN_EPS) * g + b


def _conv_base(shift):
    return -(-2 * shift // SUBLANES) * SUBLANES


def _load_prev(buf, prev_ref, carry_ref, i, tps, shift):
    p = 2 * shift
    base = _conv_base(shift)
    if prev_ref is not None:
        buf[base - p:base, :] = prev_ref[...]
    else:
        @pl.when(i % tps == 0)
        def _():
            buf[base - p:base, :] = jnp.zeros((p, buf.shape[1]), F32)

        @pl.when(i % tps != 0)
        def _():
            buf[base - p:base, :] = carry_ref[...]


def _conv_chunk(buf, u, w, r0, rows, shift):
    base = _conv_base(shift)
    buf[base + r0:base + r0 + rows, :] = u
    y = w[0:1, :] * buf[base - 2 * shift + r0:base - 2 * shift + r0 + rows, :]
    y = y + w[1:2, :] * buf[base - shift + r0:base - shift + r0 + rows, :]
    return y + w[2:3, :] * u


def _proj_mix_kernel(*refs, tm, chunk, shift, tps, has_prev):
    if has_prev:
        x_ref, wh_ref, wb_ref, wc_ref, wconv_ref, prev_ref, xb_ref, ya_ref, cm_ref, ubuf = refs
        carry_ref = None
    else:
        x_ref, wh_ref, wb_ref, wc_ref, wconv_ref, xb_ref, ya_ref, cm_ref, ubuf, carry_ref = refs
        prev_ref = None
    i, j = pl.program_id(0), pl.program_id(1)

    @pl.when(j == 0)
    def _():
        xb_ref[...] = x_ref[...].astype(BF16)

    _load_prev(ubuf, prev_ref, None if carry_ref is None else carry_ref.at[j], i, tps, shift)
    w = wconv_ref[...]
    for r0 in range(0, tm, chunk):
        xb = xb_ref[r0:r0 + chunk, :]
        u = _dot(xb, wc_ref[...]) * _dot(xb, wh_ref[...])
        y = _conv_chunk(ubuf, u, w, r0, chunk, shift)
        ya_ref[r0:r0 + chunk, :] = (_dot(xb, wb_ref[...]) * y).astype(BF16)
    base = _conv_base(shift)
    last = ubuf[base + tm - 2 * shift:base + tm, :]
    cm_ref[0] = last
    if carry_ref is not None:
        carry_ref[j] = last


def _proj_mix(x2d, w_inb, w_conv, prev, *, tm, chunk, shift, tps):
    m = x2d.shape[0]
    c = w_conv.shape[1]
    tn = COL_TILE
    nj = c // tn
    has_prev = prev is not None
    p = 2 * shift
    wspec = lambda blk: pl.BlockSpec((D_MODEL, tn), lambda i, j: (0, j + blk))
    in_specs = [pl.BlockSpec((tm, D_MODEL), lambda i, j: (i, 0)),
                wspec(_BLK_H), wspec(_BLK_BG), wspec(_BLK_CG),
                pl.BlockSpec((3, tn), lambda i, j: (0, j))]
    args = [x2d, w_inb, w_inb, w_inb, w_conv]
    scratch = [pltpu.VMEM((_conv_base(shift) + tm, tn), F32)]
    if has_prev:
        in_specs.append(pl.BlockSpec((p, tn), lambda i, j: (0, j)))
        args.append(prev)
    else:
        scratch.append(pltpu.VMEM((nj, p, tn), F32))
    return pl.pallas_call(
        functools.partial(_proj_mix_kernel, tm=tm, chunk=chunk, shift=shift, tps=tps, has_prev=has_prev),
        grid=(m // tm, nj),
        in_specs=in_specs,
        out_specs=[
            pl.BlockSpec((tm, D_MODEL), lambda i, j: (i, 0)),
            pl.BlockSpec((tm, tn), lambda i, j: (i, j)),
            pl.BlockSpec((1, p, tn), lambda i, j: (i, 0, j)),
        ],
        out_shape=[
            jax.ShapeDtypeStruct((m, D_MODEL), BF16),
            jax.ShapeDtypeStruct((m, c), BF16),
            jax.ShapeDtypeStruct((m // tm, p, c), F32),
        ],
        scratch_shapes=scratch,
        compiler_params=_cparams(2),
        name="proj_mix",
    )(*args)


def _deinterleave(slab, slab2, idx, chunk, dil):
    sub = chunk // dil
    if dil == DEINT_STEP:
        for rr in range(dil):
            yield rr, slab[idx, pl.ds(rr, sub, stride=dil), :]
        return
    quarter = chunk // DEINT_STEP
    for lo in range(DEINT_STEP):
        slab2[idx, lo * quarter:(lo + 1) * quarter, :] = slab[idx, pl.ds(lo, quarter, stride=DEINT_STEP), :]
    for lo in range(DEINT_STEP):
        for hi in range(dil // DEINT_STEP):
            yield hi * DEINT_STEP + lo, slab2[idx, pl.ds(lo * quarter + hi, sub, stride=DEINT_STEP), :]


def _proj_qkv_kernel(xb_ref, wq_ref, wk_ref, wv_ref, q_ref, k_ref, v_ref, kv_ref, slab, slab2,
                     *, tm, chunk, dil, keep):
    outs = (q_ref, k_ref, v_ref)
    sub = chunk // dil
    for r0 in range(0, tm, chunk):
        xb = xb_ref[r0:r0 + chunk, :]
        res = [_dot(xb, w[...]) for w in (wq_ref, wk_ref, wv_ref)]
        if dil == 1:
            for o_ref, r in zip(outs, res):
                o_ref[0, 0, r0:r0 + chunk, :] = r.astype(BF16)
        else:
            for a, o_ref in enumerate(outs):
                for s in range(HEADS):
                    idx = a * HEADS + s
                    slab[idx] = res[a][:, s * LANES:(s + 1) * LANES]
                    for rr, rows in _deinterleave(slab, slab2, idx, chunk, dil):
                        o_ref[0, rr, r0 // dil:r0 // dil + sub, s * LANES:(s + 1) * LANES] = rows.astype(BF16)
        lo = max(r0, tm - keep)
        if lo < r0 + chunk:
            n = r0 + chunk - lo
            for a in (1, 2):
                for h in range(HEADS):
                    start = (lo - (tm - keep)) * KV_ROWS + (a - 1) * HEADS + h
                    kv_ref[pl.ds(start, n, stride=KV_ROWS), :] = (
                        res[a][lo - r0:chunk, h * LANES:(h + 1) * LANES])


def _proj_qkv(xb, w_inb, g, *, nseq, seqlen, tm, chunk):
    win, dil = GROUPS[g]
    keep_seq = min(win, seqlen)
    keep = min(keep_seq, tm)
    tps = seqlen // tm
    n_u = seqlen // dil
    wspec = lambda blk: pl.BlockSpec((D_MODEL, COL_TILE), lambda i: (0, blk + g))
    ospec = pl.BlockSpec((1, dil, tm // dil, GROUP_W), lambda i: (i // tps, 0, i % tps, 0))
    oshape = jax.ShapeDtypeStruct((nseq, dil, n_u, GROUP_W), BF16)
    kv_index = (lambda i: (i, 0)) if keep_seq == seqlen else (lambda i: (i // tps, 0))
    q, k, v, kv = pl.pallas_call(
        functools.partial(_proj_qkv_kernel, tm=tm, chunk=chunk, dil=dil, keep=keep),
        grid=(nseq * tps,),
        in_specs=[pl.BlockSpec((tm, D_MODEL), lambda i: (i, 0)),
                  wspec(_BLK_Q), wspec(_BLK_K), wspec(_BLK_V)],
        out_specs=[ospec, ospec, ospec, pl.BlockSpec((keep * KV_ROWS, LANES), kv_index)],
        out_shape=[oshape, oshape, oshape,
                   jax.ShapeDtypeStruct((nseq * keep_seq * KV_ROWS, LANES), F32)],
        scratch_shapes=[pltpu.VMEM((3 * HEADS, chunk, LANES), F32)] * 2,
        compiler_params=_cparams(1),
        name=f"proj_qkv_dil{dil}",
    )(xb, w_inb, w_inb, w_inb)
    to_seq = lambda a: a.reshape(nseq, seqlen, GROUP_W)
    return to_seq(q), to_seq(k), to_seq(v), kv.reshape(nseq, keep_seq, 2, HEADS, HEAD_DIM)


def _proj_gate_kernel(xb_ref, w_ref, o_ref, *, tm, chunk):
    for r0 in range(0, tm, chunk):
        z = _dot(xb_ref[r0:r0 + chunk, :], w_ref[...])
        o_ref[r0:r0 + chunk, :] = _sigmoid(z).astype(BF16)


def _proj_gate(xb, w_gate, *, tm, tn, chunk):
    m = xb.shape[0]
    n = w_gate.shape[1]
    return pl.pallas_call(
        functools.partial(_proj_gate_kernel, tm=tm, chunk=chunk),
        grid=(n // tn, m // tm),
        in_specs=[
            pl.BlockSpec((tm, D_MODEL), lambda c, i: (i, 0)),
            pl.BlockSpec((D_MODEL, tn), lambda c, i: (0, c)),
        ],
        out_specs=pl.BlockSpec((tm, tn), lambda c, i: (i, c)),
        out_shape=jax.ShapeDtypeStruct((m, n), BF16),
        compiler_params=_cparams(2),
        name="proj_gate",
    )(xb, w_gate)


ATTN_GROUP_ORDER = (2, 1, 0)
ATTN_UNROLL = 4


def _attn_block(qkv_refs, bias_ref, yb_ref, state, g, row0, pos0, first):
    q_ref, k_ref, v_ref = qkv_refs
    acc, m_s, l_s = state
    dil = GROUPS[g][1]
    opens, closes = g == ATTN_GROUP_ORDER[0], g == ATTN_GROUP_ORDER[-1]
    rows = pl.ds(pl.multiple_of(pos0, BAND), BAND) if dil == 1 else pl.ds(pos0, BAND, stride=dil)
    qrows = pl.ds(row0, BAND)
    krows = qrows if first else pl.ds(row0 - BAND, 2 * BAND)
    heads = lambda ref, r: jnp.stack([ref[0, r, h * HEAD_DIM:(h + 1) * HEAD_DIM] for h in range(HEADS)])
    q, kw, vw = heads(q_ref, qrows), heads(k_ref, krows), heads(v_ref, krows)
    bias = bias_ref[g, :, :, BAND:] if first else bias_ref[g]
    s = jnp.einsum("hqd,hkd->hqk", q, kw, preferred_element_type=F32) * ATT_SCALE + bias
    mx = jnp.max(s, axis=-1, keepdims=True)
    p = jnp.exp(s - mx)
    num = jnp.einsum("hqk,hkd->hqd", p.astype(BF16), vw, preferred_element_type=F32)
    wide = (HEADS, BAND, HEAD_DIM)
    mx = jnp.broadcast_to(mx, wide)
    den = jnp.broadcast_to(jnp.sum(p, axis=-1, keepdims=True), wide)
    if not opens:
        m_old = m_s[:, rows, :]
        m_new = jnp.maximum(m_old, mx)
        w_old, w_new = jnp.exp(m_old - m_new), jnp.exp(mx - m_new)
        num = acc[:, rows, :] * w_old + num * w_new
        den = l_s[:, rows, :] * w_old + den * w_new
        mx = m_new
    if closes:
        out = (num / den).astype(BF16)
        for h in range(HEADS):
            yb_ref[rows, h * HEAD_DIM:(h + 1) * HEAD_DIM] = out[h]
    else:
        acc[:, rows, :] = num
        m_s[:, rows, :] = mx
        l_s[:, rows, :] = den


def _attn_kernel(*refs, seqlen):
    qkv = refs[0:9]
    bias_ref, yb_ref = refs[9:11]
    state = refs[11:14]
    for g in ATTN_GROUP_ORDER:
        dil = GROUPS[g][1]
        n_u = seqlen // dil
        nblk = n_u // BAND
        block = functools.partial(_attn_block, qkv[3 * g:3 * g + 3], bias_ref, yb_ref, state, g)

        def per_residue(r, carry, block=block, n_u=n_u, nblk=nblk, dil=dil):
            base = pl.multiple_of(r * n_u, BAND)
            block(base, r, True)

            def later(qb, c):
                block(pl.multiple_of(base + qb * BAND, BAND), qb * (BAND * dil) + r, False)
                return c

            if nblk <= ATTN_UNROLL:
                for qb in range(1, nblk):
                    later(qb, carry)
                return carry
            return lax.fori_loop(1, nblk, later, carry, unroll=ATTN_UNROLL // 2)

        lax.fori_loop(0, dil, per_residue, 0, unroll=ATTN_UNROLL if nblk == 1 else 1)


def _attention(qkv, bias, *, nseq, seqlen):
    spec = pl.BlockSpec((1, seqlen, GROUP_W), lambda b: (b, 0, 0))
    return pl.pallas_call(
        functools.partial(_attn_kernel, seqlen=seqlen),
        grid=(nseq,),
        in_specs=[spec] * 9 + [pl.BlockSpec(bias.shape, lambda b: (0, 0, 0, 0))],
        out_specs=pl.BlockSpec((seqlen, GROUP_W), lambda b: (b, 0)),
        out_shape=jax.ShapeDtypeStruct((nseq * seqlen, GROUP_W), BF16),
        scratch_shapes=[pltpu.VMEM((HEADS, seqlen, HEAD_DIM), F32)] * 3,
        compiler_params=_cparams(1),
        name="attention",
    )(*qkv, bias)


def _mix_kernel(ya_ref, yb_ref, ga_ref, gb_ref, x_ref, wa_ref, wb_ref, wo_ref, g_ref, b_ref,
                x1_ref, x1b_ref, *, tm, chunk):
    for r0 in range(0, tm, chunk):
        rows = slice(r0, r0 + chunk)
        a = _dot(ya_ref[rows, :], wa_ref[...])
        b = _dot(yb_ref[rows, :], wb_ref[...])
        mixed = ga_ref[rows, :].astype(F32) * a + gb_ref[rows, :].astype(F32) * b
        z = ALPHA * x_ref[rows, :] + _dot(mixed.astype(BF16), wo_ref[...])
        out = _layernorm(z, g_ref[...], b_ref[...])
        x1_ref[rows, :] = out
        x1b_ref[rows, :] = out.astype(BF16)


def _mix(ya, yb, sg, x2d, w_a, w_b, w_o, ln_g, ln_b, *, tm, chunk):
    m = x2d.shape[0]
    row = lambda w, c=0: pl.BlockSpec((tm, w), lambda i: (i, c))
    const = lambda shape: pl.BlockSpec(shape, lambda i: (0, 0), pipeline_mode=pl.Buffered(1))
    return pl.pallas_call(
        functools.partial(_mix_kernel, tm=tm, chunk=chunk),
        grid=(m // tm,),
        in_specs=[row(D_MODEL), row(GROUP_W), row(D_MODEL, 0), row(D_MODEL, 1), row(D_MODEL),
                  const((D_MODEL, D_MODEL)), const((GROUP_W, D_MODEL)), const((D_MODEL, D_MODEL)),
                  const((1, D_MODEL)), const((1, D_MODEL))],
        out_specs=[row(D_MODEL), row(D_MODEL)],
        out_shape=[jax.ShapeDtypeStruct((m, D_MODEL), F32), jax.ShapeDtypeStruct((m, D_MODEL), BF16)],
        compiler_params=_cparams(1),
        name="mix_out",
    )(ya, yb, sg, sg, x2d, w_a, w_b, w_o, ln_g, ln_b)


def _ffn_up_kernel(*refs, tm, chunk, shift, tps, has_prev):
    if has_prev:
        x1b_ref, wa_ref, wb_ref, wconv_ref, prev_ref, hm_ref, cf_ref, abuf = refs
        carry_ref = None
    else:
        x1b_ref, wa_ref, wb_ref, wconv_ref, hm_ref, cf_ref, abuf, carry_ref = refs
        prev_ref = None
    i = pl.program_id(1)
    _load_prev(abuf, prev_ref, carry_ref, i, tps, shift)
    w = wconv_ref[...]
    for r0 in range(0, tm, chunk):
        xb = x1b_ref[r0:r0 + chunk, :]
        ac = _conv_chunk(abuf, _dot(xb, wa_ref[...]), w, r0, chunk, shift)
        hm_ref[r0:r0 + chunk, :] = (ac * _sigmoid(ac) * _dot(xb, wb_ref[...])).astype(BF16)
    base = _conv_base(shift)
    last = abuf[base + tm - 2 * shift:base + tm, :]
    cf_ref[0] = last
    if carry_ref is not None:
        carry_ref[...] = last


def _ffn_up(x1b, w_upb, w_conv, prev, *, tm, chunk, shift, tps):
    m = x1b.shape[0]
    tf = COL_TILE
    nf = D_FF // tf
    has_prev = prev is not None
    p = 2 * shift
    in_specs = [
        pl.BlockSpec((tm, D_MODEL), lambda f, i: (i, 0)),
        pl.BlockSpec((D_MODEL, tf), lambda f, i: (0, f)),
        pl.BlockSpec((D_MODEL, tf), lambda f, i: (0, f + nf)),
        pl.BlockSpec((3, tf), lambda f, i: (0, f)),
    ]
    args = [x1b, w_upb, w_upb, w_conv]
    scratch = [pltpu.VMEM((_conv_base(shift) + tm, tf), F32)]
    if has_prev:
        in_specs.append(pl.BlockSpec((p, tf), lambda f, i: (0, f)))
        args.append(prev)
    else:
        scratch.append(pltpu.VMEM((p, tf), F32))
    return pl.pallas_call(
        functools.partial(_ffn_up_kernel, tm=tm, chunk=chunk, shift=shift, tps=tps, has_prev=has_prev),
        grid=(nf, m // tm),
        in_specs=in_specs,
        out_specs=[pl.BlockSpec((tm, tf), lambda f, i: (i, f)),
                   pl.BlockSpec((1, p, tf), lambda f, i: (i, 0, f))],
        out_shape=[jax.ShapeDtypeStruct((m, D_FF), BF16),
                   jax.ShapeDtypeStruct((m // tm, p, D_FF), F32)],
        scratch_shapes=scratch,
        compiler_params=_cparams(2),
        name="ffn_up",
    )(*args)


def _ffn_down_kernel(hm_ref, x1_ref, wd_ref, g_ref, b_ref, y_ref, *, tm, chunk):
    for r0 in range(0, tm, chunk):
        rows = slice(r0, r0 + chunk)
        z = ALPHA * x1_ref[rows, :] + _dot(hm_ref[rows, :], wd_ref[...])
        y_ref[rows, :] = _layernorm(z, g_ref[...], b_ref[...])


def _ffn_down(hm, x1, w_dnb, ln_g, ln_b, *, tm, chunk):
    m = x1.shape[0]
    const = lambda shape: pl.BlockSpec(shape, lambda i: (0, 0), pipeline_mode=pl.Buffered(1))
    return pl.pallas_call(
        functools.partial(_ffn_down_kernel, tm=tm, chunk=chunk),
        grid=(m // tm,),
        in_specs=[pl.BlockSpec((tm, D_FF), lambda i: (i, 0)),
                  pl.BlockSpec((tm, D_MODEL), lambda i: (i, 0)),
                  const((D_FF, D_MODEL)), const((1, D_MODEL)), const((1, D_MODEL))],
        out_specs=pl.BlockSpec((tm, D_MODEL), lambda i: (i, 0)),
        out_shape=jax.ShapeDtypeStruct((m, D_MODEL), F32),
        compiler_params=_cparams(1),
        name="ffn_down",
    )(hm, x1, w_dnb, ln_g, ln_b)


def _sample_qkv_kernel(xb_ref, wq_ref, wk_ref, wv_ref, q_ref, kvn_ref, *, nb, n_new):
    g = pl.program_id(0)
    xb = xb_ref[...]
    res = [_dot(xb, w[...]) for w in (wq_ref, wk_ref, wv_ref)]
    q_stride = n_new * N_GROUPS * HEADS
    kv_stride = n_new * KV_ROWS
    for t in range(n_new):
        for h in range(HEADS):
            piece = lambda r: r[t * nb:(t + 1) * nb, h * LANES:(h + 1) * LANES]
            q_ref[pl.ds(t * N_GROUPS * HEADS + g * HEADS + h, nb, stride=q_stride), :] = piece(res[0])
            for a in (1, 2):
                start = g * nb * kv_stride + t * KV_ROWS + (a - 1) * HEADS + h
                kvn_ref[pl.ds(start, nb, stride=kv_stride), :] = piece(res[a])


def _sample_qkv(xb, w_inb, *, nb, n_new):
    ms = nb * n_new
    wspec = lambda blk: pl.BlockSpec((D_MODEL, COL_TILE), lambda g: (0, blk + g))
    q_rows = ms * N_GROUPS * HEADS
    kv_rows = N_GROUPS * ms * KV_ROWS
    q, kvn = pl.pallas_call(
        functools.partial(_sample_qkv_kernel, nb=nb, n_new=n_new),
        grid=(N_GROUPS,),
        in_specs=[pl.BlockSpec((ms, D_MODEL), lambda g: (0, 0)),
                  wspec(_BLK_Q), wspec(_BLK_K), wspec(_BLK_V)],
        out_specs=[pl.BlockSpec((q_rows, LANES), lambda g: (0, 0)),
                   pl.BlockSpec((kv_rows, LANES), lambda g: (0, 0))],
        out_shape=[jax.ShapeDtypeStruct((q_rows, LANES), F32),
                   jax.ShapeDtypeStruct((kv_rows, LANES), F32)],
        compiler_params=_cparams(1),
        name="sample_qkv",
    )(xb, w_inb, w_inb, w_inb)
    return (q.reshape(nb, n_new, N_GROUPS, HEADS, HEAD_DIM),
            kvn.reshape(N_GROUPS, nb, n_new, 2, HEADS, HEAD_DIM))


def _sample_attn_kernel(slope_ref, q_ref, c0_ref, c1_ref, c2_ref, kvn_ref, o_ref, *, n_new):
    c_refs = (c0_ref, c1_ref, c2_ref)
    m_idx = lax.broadcasted_iota(jnp.int32, (BAND, 1, 1), 0)
    n_idx = lax.broadcasted_iota(jnp.int32, (n_new, 1, 1), 0)
    for t in range(n_new):
        scores, values = [], []
        for g, (_, dil) in enumerate(GROUPS):
            slope = slope_ref[g][None, :, 0:1]
            q = q_ref[0, t, g][None]
            res = 0 if dil == 1 else t
            kc = c_refs[g][0, :, res, 0]
            vc = c_refs[g][0, :, res, 1]
            jc = (BAND + t - m_idx) if dil == 1 else (BAND - m_idx)
            sc = jnp.sum(kc * q, axis=-1, keepdims=True) * ATT_SCALE - slope * (dil * jc).astype(F32)
            if dil == 1:
                sc = jnp.where(jc <= BAND, sc, -jnp.inf)
                kn, vn = kvn_ref[g, 0, :, 0], kvn_ref[g, 0, :, 1]
                jn = t - n_idx
                sn = jnp.sum(kn * q, axis=-1, keepdims=True) * ATT_SCALE - slope * jn.astype(F32)
                sn = jnp.where(jn >= 0, sn, -jnp.inf)
            else:
                kn, vn = kvn_ref[g, 0, t:t + 1, 0], kvn_ref[g, 0, t:t + 1, 1]
                sn = jnp.sum(kn * q, axis=-1, keepdims=True) * ATT_SCALE
            scores += [sc, sn]
            values += [vc, vn]
        mx = functools.reduce(jnp.maximum, [jnp.max(s, axis=0, keepdims=True) for s in scores])
        ps = [jnp.exp(s - mx) for s in scores]
        den = functools.reduce(jnp.add, [jnp.sum(p, axis=0, keepdims=True) for p in ps])
        acc = functools.reduce(jnp.add, [jnp.sum(p * v, axis=0, keepdims=True) for p, v in zip(ps, values)])
        o_ref[0, t] = (acc / den)[0]


def _sample_attention(slopes, q, caches, kvn):
    nb, n_new = q.shape[0], q.shape[1]
    c_args, c_specs = [], []
    for (win, dil), c in zip(GROUPS, caches):
        res = min(dil, n_new)
        c_args.append(c.reshape(nb, win // dil, dil, 2, HEADS, HEAD_DIM))
        c_specs.append(pl.BlockSpec((1, win // dil, res, 2, HEADS, HEAD_DIM), lambda b: (b, 0, 0, 0, 0, 0)))
    return pl.pallas_call(
        functools.partial(_sample_attn_kernel, n_new=n_new),
        grid=(nb,),
        in_specs=[pl.BlockSpec(slopes.shape, lambda b: (0, 0, 0)),
                  pl.BlockSpec((1, n_new, N_GROUPS, HEADS, HEAD_DIM), lambda b: (b, 0, 0, 0, 0))]
        + c_specs
        + [pl.BlockSpec((N_GROUPS, 1, n_new, 2, HEADS, HEAD_DIM), lambda b: (0, b, 0, 0, 0, 0))],
        out_specs=pl.BlockSpec((1, n_new, HEADS, HEAD_DIM), lambda b: (b, 0, 0, 0)),
        out_shape=jax.ShapeDtypeStruct((nb, n_new, HEADS, HEAD_DIM), F32),
        compiler_params=_cparams(1),
        name="sample_attn",
    )(slopes, q, *c_args, kvn)


def _cache_copies(refs):
    c_refs, kvn_ref, o_refs, sem = refs[0:3], refs[3], refs[4:7], refs[7]
    n_new = kvn_ref.shape[2]
    copies = []
    for g in range(N_GROUPS):
        keep = c_refs[g].shape[1] - n_new
        copies.append(pltpu.make_async_copy(
            c_refs[g].at[:, pl.ds(n_new, keep)], o_refs[g].at[:, pl.ds(0, keep)], sem.at[2 * g]))
        copies.append(pltpu.make_async_copy(
            kvn_ref.at[g], o_refs[g].at[:, pl.ds(keep, n_new)], sem.at[2 * g + 1]))
    return copies


def _shift_cache_kernel(*refs):
    copies = _cache_copies(refs)
    for c in copies:
        c.start()
    for c in copies:
        c.wait()


def _shift_caches(caches, kvn):
    any_spec = pl.BlockSpec(memory_space=pl.ANY)
    return pl.pallas_call(
        _shift_cache_kernel,
        in_specs=[any_spec] * 4,
        out_specs=[any_spec] * 3,
        out_shape=[jax.ShapeDtypeStruct(c.shape, c.dtype) for c in caches],
        scratch_shapes=[pltpu.SemaphoreType.DMA((2 * N_GROUPS,))],
        name="shift_caches",
    )(*caches, kvn)


def _alibi_slopes():
    h = jnp.arange(1, N_GROUPS * HEADS + 1, dtype=F32)
    return jnp.exp2(-8.0 * h / (N_GROUPS * HEADS)).reshape(N_GROUPS, HEADS)


def _prompt_bias(slopes):
    qi = jnp.arange(BAND)[:, None]
    kj = jnp.arange(2 * BAND)[None, :]
    du = BAND + qi - kj
    valid = (du >= 0) & (du <= BAND)
    tables = []
    for g, (_, dil) in enumerate(GROUPS):
        pen = slopes[g][:, None, None] * (dil * du).astype(F32)[None]
        tables.append(jnp.where(valid[None], -pen, -jnp.inf))
    return jnp.stack(tables)


def _channel_layers(ya, yb, sg, x2d, ffn_state, weights, *, tm_mix, tm_up, tm_down, chunk, shift, tps):
    w_a, w_b, w_o, w_upb, w_dnb, w_conv_ffn, ln1, ln2 = weights
    x1, x1b = _mix(ya, yb, sg, x2d, w_a, w_b, w_o, *ln1, tm=tm_mix, chunk=min(chunk, tm_mix))
    hm, cf = _ffn_up(x1b, w_upb, w_conv_ffn, ffn_state, tm=tm_up, chunk=min(2 * chunk, tm_up),
                     shift=shift, tps=tps)
    y = _ffn_down(hm, x1, w_dnb, *ln2, tm=tm_down, chunk=min(chunk, tm_down))
    return y, cf


def kernel(x_prompt, x_sample, state_conv_mix, cache_kv0, cache_kv1, cache_kv2, state_conv_ffn,
           w_in, w_conv_mix, w_branch_a, w_branch_b, w_out, ln1_g, ln1_b,
           w_up, w_conv_ffn, w_down, ln2_g, ln2_b):
    nb_p, seq, _ = x_prompt.shape
    nb_s, n_new, _ = x_sample.shape
    caches = (cache_kv0, cache_kv1, cache_kv2)

    n_proj = _BLK_GATE * COL_TILE
    w_inb = w_in[:, :n_proj].astype(BF16)
    w_gate = w_in[:, n_proj:].astype(BF16)
    weights = (w_branch_a.astype(BF16), w_branch_b.astype(BF16), w_out.astype(BF16),
               w_up.astype(BF16), w_down.astype(BF16), w_conv_ffn,
               (ln1_g.reshape(1, -1), ln1_b.reshape(1, -1)), (ln2_g.reshape(1, -1), ln2_b.reshape(1, -1)))
    slopes = _alibi_slopes()

    tm = 1024
    tps = seq // tm
    x2d = x_prompt.reshape(nb_p * seq, D_MODEL)
    xb, ya, cm_p = _proj_mix(x2d, w_inb, w_conv_mix, None, tm=tm, chunk=256, shift=1, tps=tps)
    qkv, kv_p = [], []
    for g in range(N_GROUPS):
        q, k, v, kv = _proj_qkv(xb, w_inb, g, nseq=nb_p, seqlen=seq, tm=tm, chunk=256)
        qkv += [q, k, v]
        kv_p.append(kv)
    sg = _proj_gate(xb, w_gate, tm=tm, tn=1024, chunk=256)
    yb = _attention(qkv, _prompt_bias(slopes), nseq=nb_p, seqlen=seq)
    y2d, cf_p = _channel_layers(ya, yb, sg, x2d, None, weights,
                                tm_mix=256, tm_up=tm, tm_down=256, chunk=128, shift=1, tps=tps)
    y_prompt = y2d.reshape(nb_p, seq, D_MODEL)
    cm_p = cm_p[tps - 1::tps]
    cf_p = cf_p[tps - 1::tps]

    ms = nb_s * n_new
    to_tm = lambda a: jnp.swapaxes(a, 0, 1).reshape(a.shape[0] * a.shape[1], a.shape[2])
    from_tm = lambda a, t: jnp.swapaxes(a.reshape(t, nb_s, a.shape[-1]), 0, 1)
    xs = to_tm(x_sample)
    xbs, yas, cm_s = _proj_mix(xs, w_inb, w_conv_mix, to_tm(state_conv_mix),
                               tm=ms, chunk=ms, shift=nb_s, tps=1)
    q_s, kvn = _sample_qkv(xbs, w_inb, nb=nb_s, n_new=n_new)
    sgs = _proj_gate(xbs, w_gate, tm=ms, tn=1024, chunk=ms)
    slope_rows = jnp.broadcast_to(slopes[:, :, None], (N_GROUPS, HEADS, HEAD_DIM))
    yb_s = _sample_attention(slope_rows, q_s, caches, kvn)
    ybs = to_tm(yb_s.reshape(nb_s, n_new, GROUP_W)).astype(BF16)
    ys, cf_s = _channel_layers(yas, ybs, sgs, xs, to_tm(state_conv_ffn), weights,
                               tm_mix=ms, tm_up=ms, tm_down=ms, chunk=ms, shift=nb_s, tps=1)
    y_sample = from_tm(ys, n_new)
    kv_s = _shift_caches(caches, kvn)

    return (y_prompt, y_sample, cm_p, kv_p[0], kv_p[1], kv_p[2], cf_p,
            from_tm(cm_s[0], 2), kv_s[0], kv_s[1], kv_s[2], from_tm(cf_s[0], 2))
```

```python
import functools

import jax
import jax.numpy as jnp
from jax import lax
from jax.experimental import pallas as pl
from jax.experimental.pallas import tpu as pltpu

F32 = jnp.float32
BF16 = jnp.bfloat16

D_MODEL = 2048
HEAD_DIM = 128
HEADS = 4
GROUPS = ((128, 1), (512, 4), (2048, 16))
N_GROUPS = 3
GROUP_W = HEADS * HEAD_DIM
KV_ROWS = 2 * HEADS
BAND = 128
ATT_SCALE = HEAD_DIM ** -0.5
D_FF = 5632
LN_EPS = 1e-5
ALPHA = 2.0 ** 0.25

SUBLANES = 8
LANES = 128
COL_TILE = 512
DEINT_STEP = 4
V7X_VMEM_LIMIT = 58 * 1024 * 1024

_BLK_H, _BLK_BG, _BLK_CG = 0, 4, 8
_BLK_Q, _BLK_K, _BLK_V = 12, 15, 18
_BLK_GATE = 21


def _cparams(n_axes, vmem=V7X_VMEM_LIMIT):
    return pltpu.CompilerParams(dimension_semantics=("arbitrary",) * n_axes, vmem_limit_bytes=vmem)


def _dot(a, b):
    return jnp.dot(a, b, preferred_element_type=F32)


def _sigmoid(z):
    return 1.0 / (1.0 + jnp.exp(-z))


def _layernorm(z, g, b):
    mu = jnp.mean(z, axis=-1, keepdims=True)
    zc = z - mu
    var = jnp.mean(zc * zc, axis=-1, keepdims=True)
    return zc * lax.rsqrt(var + LN_EPS) * g + b


def _conv_base(shift):
    return -(-2 * shift // SUBLANES) * SUBLANES


def _load_prev(buf, prev_ref, carry_ref, i, tps, shift):
    p = 2 * shift
    base = _conv_base(shift)
    if prev_ref is not None:
        buf[base - p:base, :] = prev_ref[...]
    else:
        @pl.when(i % tps == 0)
        def _():
            buf[base - p:base, :] = jnp.zeros((p, buf.shape[1]), F32)

        @pl.when(i % tps != 0)
        def _():
            buf[base - p:base, :] = carry_ref[...]


def _conv_chunk(buf, u, w, r0, rows, shift):
    base = _conv_base(shift)
    buf[base + r0:base + r0 + rows, :] = u
    y = w[0:1, :] * buf[base - 2 * shift + r0:base - 2 * shift + r0 + rows, :]
    y = y + w[1:2, :] * buf[base - shift + r0:base - shift + r0 + rows, :]
    return y + w[2:3, :] * u


def _proj_mix_kernel(*refs, tm, chunk, shift, tps, has_prev):
    if has_prev:
        x_ref, wh_ref, wb_ref, wc_ref, wconv_ref, prev_ref, xb_ref, ya_ref, cm_ref, ubuf = refs
        carry_ref = None
    else:
        x_ref, wh_ref, wb_ref, wc_ref, wconv_ref, xb_ref, ya_ref, cm_ref, ubuf, carry_ref = refs
        prev_ref = None
    i, j = pl.program_id(0), pl.program_id(1)

    @pl.when(j == 0)
    def _():
        xb_ref[...] = x_ref[...].astype(BF16)

    _load_prev(ubuf, prev_ref, None if carry_ref is None else carry_ref.at[j], i, tps, shift)
    w = wconv_ref[...]
    for r0 in range(0, tm, chunk):
        xb = xb_ref[r0:r0 + chunk, :]
        u = _dot(xb, wc_ref[...]) * _dot(xb, wh_ref[...])
        y = _conv_chunk(ubuf, u, w, r0, chunk, shift)
        ya_ref[r0:r0 + chunk, :] = (_dot(xb, wb_ref[...]) * y).astype(BF16)
    base = _conv_base(shift)
    last = ubuf[base + tm - 2 * shift:base + tm, :]
    cm_ref[0] = last
    if carry_ref is not None:
        carry_ref[j] = last


def _proj_mix(x2d, w_inb, w_conv, prev, *, tm, chunk, shift, tps):
    m = x2d.shape[0]
    c = w_conv.shape[1]
    tn = COL_TILE
    nj = c // tn
    has_prev = prev is not None
    p = 2 * shift
    wspec = lambda blk: pl.BlockSpec((D_MODEL, tn), lambda i, j: (0, j + blk))
    in_specs = [pl.BlockSpec((tm, D_MODEL), lambda i, j: (i, 0)),
                wspec(_BLK_H), wspec(_BLK_BG), wspec(_BLK_CG),
                pl.BlockSpec((3, tn), lambda i, j: (0, j))]
    args = [x2d, w_inb, w_inb, w_inb, w_conv]
    scratch = [pltpu.VMEM((_conv_base(shift) + tm, tn), F32)]
    if has_prev:
        in_specs.append(pl.BlockSpec((p, tn), lambda i, j: (0, j)))
        args.append(prev)
    else:
        scratch.append(pltpu.VMEM((nj, p, tn), F32))
    return pl.pallas_call(
        functools.partial(_proj_mix_kernel, tm=tm, chunk=chunk, shift=shift, tps=tps, has_prev=has_prev),
        grid=(m // tm, nj),
        in_specs=in_specs,
        out_specs=[
            pl.BlockSpec((tm, D_MODEL), lambda i, j: (i, 0)),
            pl.BlockSpec((tm, tn), lambda i, j: (i, j)),
            pl.BlockSpec((1, p, tn), lambda i, j: (i, 0, j)),
        ],
        out_shape=[
            jax.ShapeDtypeStruct((m, D_MODEL), BF16),
            jax.ShapeDtypeStruct((m, c), BF16),
            jax.ShapeDtypeStruct((m // tm, p, c), F32),
        ],
        scratch_shapes=scratch,
        compiler_params=_cparams(2),
        name="proj_mix",
    )(*args)


def _deinterleave(slab, slab2, idx, chunk, dil):
    sub = chunk // dil
    if dil == DEINT_STEP:
        for rr in range(dil):
            yield rr, slab[idx, pl.ds(rr, sub, stride=dil), :]
        return
    quarter = chunk // DEINT_STEP
    for lo in range(DEINT_STEP):
        slab2[idx, lo * quarter:(lo + 1) * quarter, :] = slab[idx, pl.ds(lo, quarter, stride=DEINT_STEP), :]
    for lo in range(DEINT_STEP):
        for hi in range(dil // DEINT_STEP):
            yield hi * DEINT_STEP + lo, slab2[idx, pl.ds(lo * quarter + hi, sub, stride=DEINT_STEP), :]


def _proj_qkv_kernel(xb_ref, wq_ref, wk_ref, wv_ref, q_ref, k_ref, v_ref, kv_ref, slab, slab2,
                     *, tm, chunk, dil, keep):
    outs = (q_ref, k_ref, v_ref)
    sub = chunk // dil
    for r0 in range(0, tm, chunk):
        xb = xb_ref[r0:r0 + chunk, :]
        res = [_dot(xb, w[...]) for w in (wq_ref, wk_ref, wv_ref)]
        if dil == 1:
            for o_ref, r in zip(outs, res):
                o_ref[0, 0, r0:r0 + chunk, :] = r.astype(BF16)
        else:
            for a, o_ref in enumerate(outs):
                for s in range(HEADS):
                    idx = a * HEADS + s
                    slab[idx] = res[a][:, s * LANES:(s + 1) * LANES]
                    for rr, rows in _deinterleave(slab, slab2, idx, chunk, dil):
                        o_ref[0, rr, r0 // dil:r0 // dil + sub, s * LANES:(s + 1) * LANES] = rows.astype(BF16)
        lo = max(r0, tm - keep)
        if lo < r0 + chunk:
            n = r0 + chunk - lo
            for a in (1, 2):
                for h in range(HEADS):
                    start = (lo - (tm - keep)) * KV_ROWS + (a - 1) * HEADS + h
                    kv_ref[pl.ds(start, n, stride=KV_ROWS), :] = (
                        res[a][lo - r0:chunk, h * LANES:(h + 1) * LANES])


def _proj_qkv(xb, w_inb, g, *, nseq, seqlen, tm, chunk):
    win, dil = GROUPS[g]
    keep_seq = min(win, seqlen)
    keep = min(keep_seq, tm)
    tps = seqlen // tm
    n_u = seqlen // dil
    wspec = lambda blk: pl.BlockSpec((D_MODEL, COL_TILE), lambda i: (0, blk + g))
    ospec = pl.BlockSpec((1, dil, tm // dil, GROUP_W), lambda i: (i // tps, 0, i % tps, 0))
    oshape = jax.ShapeDtypeStruct((nseq, dil, n_u, GROUP_W), BF16)
    kv_index = (lambda i: (i, 0)) if keep_seq == seqlen else (lambda i: (i // tps, 0))
    q, k, v, kv = pl.pallas_call(
        functools.partial(_proj_qkv_kernel, tm=tm, chunk=chunk, dil=dil, keep=keep),
        grid=(nseq * tps,),
        in_specs=[pl.BlockSpec((tm, D_MODEL), lambda i: (i, 0)),
                  wspec(_BLK_Q), wspec(_BLK_K), wspec(_BLK_V)],
        out_specs=[ospec, ospec, ospec, pl.BlockSpec((keep * KV_ROWS, LANES), kv_index)],
        out_shape=[oshape, oshape, oshape,
                   jax.ShapeDtypeStruct((nseq * keep_seq * KV_ROWS, LANES), F32)],
        scratch_shapes=[pltpu.VMEM((3 * HEADS, chunk, LANES), F32)] * 2,
        compiler_params=_cparams(1),
        name=f"proj_qkv_dil{dil}",
    )(xb, w_inb, w_inb, w_inb)
    to_seq = lambda a: a.reshape(nseq, seqlen, GROUP_W)
    return to_seq(q), to_seq(k), to_seq(v), kv.reshape(nseq, keep_seq, 2, HEADS, HEAD_DIM)


def _proj_gate_kernel(xb_ref, w_ref, o_ref, *, tm, chunk):
    for r0 in range(0, tm, chunk):
        z = _dot(xb_ref[r0:r0 + chunk, :], w_ref[...])
        o_ref[r0:r0 + chunk, :] = _sigmoid(z).astype(BF16)


def _proj_gate(xb, w_gate, *, tm, tn, chunk):
    m = xb.shape[0]
    n = w_gate.shape[1]
    return pl.pallas_call(
        functools.partial(_proj_gate_kernel, tm=tm, chunk=chunk),
        grid=(n // tn, m // tm),
        in_specs=[
            pl.BlockSpec((tm, D_MODEL), lambda c, i: (i, 0)),
            pl.BlockSpec((D_MODEL, tn), lambda c, i: (0, c)),
        ],
        out_specs=pl.BlockSpec((tm, tn), lambda c, i: (i, c)),
        out_shape=jax.ShapeDtypeStruct((m, n), BF16),
        compiler_params=_cparams(2),
        name="proj_gate",
    )(xb, w_gate)


ATTN_GROUP_ORDER = (2, 1, 0)
ATTN_UNROLL = 4


def _attn_block(qkv_refs, bias_ref, yb_ref, state, g, row0, pos0, first):
    q_ref, k_ref, v_ref = qkv_refs
    acc, m_s, l_s = state
    dil = GROUPS[g][1]
    opens, closes = g == ATTN_GROUP_ORDER[0], g == ATTN_GROUP_ORDER[-1]
    rows = pl.ds(pl.multiple_of(pos0, BAND), BAND) if dil == 1 else pl.ds(pos0, BAND, stride=dil)
    qrows = pl.ds(row0, BAND)
    krows = qrows if first else pl.ds(row0 - BAND, 2 * BAND)
    heads = lambda ref, r: jnp.stack([ref[0, r, h * HEAD_DIM:(h + 1) * HEAD_DIM] for h in range(HEADS)])
    q, kw, vw = heads(q_ref, qrows), heads(k_ref, krows), heads(v_ref, krows)
    bias = bias_ref[g, :, :, BAND:] if first else bias_ref[g]
    s = jnp.einsum("hqd,hkd->hqk", q, kw, preferred_element_type=F32) * ATT_SCALE + bias
    mx = jnp.max(s, axis=-1, keepdims=True)
    p = jnp.exp(s - mx)
    num = jnp.einsum("hqk,hkd->hqd", p.astype(BF16), vw, preferred_element_type=F32)
    wide = (HEADS, BAND, HEAD_DIM)
    mx = jnp.broadcast_to(mx, wide)
    den = jnp.broadcast_to(jnp.sum(p, axis=-1, keepdims=True), wide)
    if not opens:
        m_old = m_s[:, rows, :]
        m_new = jnp.maximum(m_old, mx)
        w_old, w_new = jnp.exp(m_old - m_new), jnp.exp(mx - m_new)
        num = acc[:, rows, :] * w_old + num * w_new
        den = l_s[:, rows, :] * w_old + den * w_new
        mx = m_new
    if closes:
        out = (num / den).astype(BF16)
        for h in range(HEADS):
            yb_ref[rows, h * HEAD_DIM:(h + 1) * HEAD_DIM] = out[h]
    else:
        acc[:, rows, :] = num
        m_s[:, rows, :] = mx
        l_s[:, rows, :] = den


def _attn_kernel(*refs, seqlen):
    qkv = refs[0:9]
    bias_ref, yb_ref = refs[9:11]
    state = refs[11:14]
    for g in ATTN_GROUP_ORDER:
        dil = GROUPS[g][1]
        n_u = seqlen // dil
        nblk = n_u // BAND
        block = functools.partial(_attn_block, qkv[3 * g:3 * g + 3], bias_ref, yb_ref, state, g)

        def per_residue(r, carry, block=block, n_u=n_u, nblk=nblk, dil=dil):
            base = pl.multiple_of(r * n_u, BAND)
            block(base, r, True)

            def later(qb, c):
                block(pl.multiple_of(base + qb * BAND, BAND), qb * (BAND * dil) + r, False)
                return c

            if nblk <= ATTN_UNROLL:
                for qb in range(1, nblk):
                    later(qb, carry)
                return carry
            return lax.fori_loop(1, nblk, later, carry, unroll=ATTN_UNROLL // 2)

        lax.fori_loop(0, dil, per_residue, 0, unroll=ATTN_UNROLL if nblk == 1 else 1)


def _attention(qkv, bias, *, nseq, seqlen):
    spec = pl.BlockSpec((1, seqlen, GROUP_W), lambda b: (b, 0, 0))
    return pl.pallas_call(
        functools.partial(_attn_kernel, seqlen=seqlen),
        grid=(nseq,),
        in_specs=[spec] * 9 + [pl.BlockSpec(bias.shape, lambda b: (0, 0, 0, 0))],
        out_specs=pl.BlockSpec((seqlen, GROUP_W), lambda b: (b, 0)),
        out_shape=jax.ShapeDtypeStruct((nseq * seqlen, GROUP_W), BF16),
        scratch_shapes=[pltpu.VMEM((HEADS, seqlen, HEAD_DIM), F32)] * 3,
        compiler_params=_cparams(1),
        name="attention",
    )(*qkv, bias)


def _mix_kernel(ya_ref, yb_ref, ga_ref, gb_ref, x_ref, wa_ref, wb_ref, wo_ref, g_ref, b_ref,
                x1_ref, x1b_ref, *, tm, chunk):
    for r0 in range(0, tm, chunk):
        rows = slice(r0, r0 + chunk)
        a = _dot(ya_ref[rows, :], wa_ref[...])
        b = _dot(yb_ref[rows, :], wb_ref[...])
        mixed = ga_ref[rows, :].astype(F32) * a + gb_ref[rows, :].astype(F32) * b
        z = ALPHA * x_ref[rows, :] + _dot(mixed.astype(BF16), wo_ref[...])
        out = _layernorm(z, g_ref[...], b_ref[...])
        x1_ref[rows, :] = out
        x1b_ref[rows, :] = out.astype(BF16)


def _mix(ya, yb, sg, x2d, w_a, w_b, w_o, ln_g, ln_b, *, tm, chunk):
    m = x2d.shape[0]
    row = lambda w, c=0: pl.BlockSpec((tm, w), lambda i: (i, c))
    const = lambda shape: pl.BlockSpec(shape, lambda i: (0, 0), pipeline_mode=pl.Buffered(1))
    return pl.pallas_call(
        functools.partial(_mix_kernel, tm=tm, chunk=chunk),
        grid=(m // tm,),
        in_specs=[row(D_MODEL), row(GROUP_W), row(D_MODEL, 0), row(D_MODEL, 1), row(D_MODEL),
                  const((D_MODEL, D_MODEL)), const((GROUP_W, D_MODEL)), const((D_MODEL, D_MODEL)),
                  const((1, D_MODEL)), const((1, D_MODEL))],
        out_specs=[row(D_MODEL), row(D_MODEL)],
        out_shape=[jax.ShapeDtypeStruct((m, D_MODEL), F32), jax.ShapeDtypeStruct((m, D_MODEL), BF16)],
        compiler_params=_cparams(1),
        name="mix_out",
    )(ya, yb, sg, sg, x2d, w_a, w_b, w_o, ln_g, ln_b)


def _ffn_up_kernel(*refs, tm, chunk, shift, tps, has_prev):
    if has_prev:
        x1b_ref, wa_ref, wb_ref, wconv_ref, prev_ref, hm_ref, cf_ref, abuf = refs
        carry_ref = None
    else:
        x1b_ref, wa_ref, wb_ref, wconv_ref, hm_ref, cf_ref, abuf, carry_ref = refs
        prev_ref = None
    i = pl.program_id(1)
    _load_prev(abuf, prev_ref, carry_ref, i, tps, shift)
    w = wconv_ref[...]
    for r0 in range(0, tm, chunk):
        xb = x1b_ref[r0:r0 + chunk, :]
        ac = _conv_chunk(abuf, _dot(xb, wa_ref[...]), w, r0, chunk, shift)
        hm_ref[r0:r0 + chunk, :] = (ac * _sigmoid(ac) * _dot(xb, wb_ref[...])).astype(BF16)
    base = _conv_base(shift)
    last = abuf[base + tm - 2 * shift:base + tm, :]
    cf_ref[0] = last
    if carry_ref is not None:
        carry_ref[...] = last


def _ffn_up(x1b, w_upb, w_conv, prev, *, tm, chunk, shift, tps):
    m = x1b.shape[0]
    tf = COL_TILE
    nf = D_FF // tf
    has_prev = prev is not None
    p = 2 * shift
    in_specs = [
        pl.BlockSpec((tm, D_MODEL), lambda f, i: (i, 0)),
        pl.BlockSpec((D_MODEL, tf), lambda f, i: (0, f)),
        pl.BlockSpec((D_MODEL, tf), lambda f, i: (0, f + nf)),
        pl.BlockSpec((3, tf), lambda f, i: (0, f)),
    ]
    args = [x1b, w_upb, w_upb, w_conv]
    scratch = [pltpu.VMEM((_conv_base(shift) + tm, tf), F32)]
    if has_prev:
        in_specs.append(pl.BlockSpec((p, tf), lambda f, i: (0, f)))
        args.append(prev)
    else:
        scratch.append(pltpu.VMEM((p, tf), F32))
    return pl.pallas_call(
        functools.partial(_ffn_up_kernel, tm=tm, chunk=chunk, shift=shift, tps=tps, has_prev=has_prev),
        grid=(nf, m // tm),
        in_specs=in_specs,
        out_specs=[pl.BlockSpec((tm, tf), lambda f, i: (i, f)),
                   pl.BlockSpec((1, p, tf), lambda f, i: (i, 0, f))],
        out_shape=[jax.ShapeDtypeStruct((m, D_FF), BF16),
                   jax.ShapeDtypeStruct((m // tm, p, D_FF), F32)],
        scratch_shapes=scratch,
        compiler_params=_cparams(2),
        name="ffn_up",
    )(*args)


def _ffn_down_kernel(hm_ref, x1_ref, wd_ref, g_ref, b_ref, y_ref, *, tm, chunk):
    for r0 in range(0, tm, chunk):
        rows = slice(r0, r0 + chunk)
        z = ALPHA * x1_ref[rows, :] + _dot(hm_ref[rows, :], wd_ref[...])
        y_ref[rows, :] = _layernorm(z, g_ref[...], b_ref[...])


def _ffn_down(hm, x1, w_dnb, ln_g, ln_b, *, tm, chunk):
    m = x1.shape[0]
    const = lambda shape: pl.BlockSpec(shape, lambda i: (0, 0), pipeline_mode=pl.Buffered(1))
    return pl.pallas_call(
        functools.partial(_ffn_down_kernel, tm=tm, chunk=chunk),
        grid=(m // tm,),
        in_specs=[pl.BlockSpec((tm, D_FF), lambda i: (i, 0)),
                  pl.BlockSpec((tm, D_MODEL), lambda i: (i, 0)),
                  const((D_FF, D_MODEL)), const((1, D_MODEL)), const((1, D_MODEL))],
        out_specs=pl.BlockSpec((tm, D_MODEL), lambda i: (i, 0)),
        out_shape=jax.ShapeDtypeStruct((m, D_MODEL), F32),
        compiler_params=_cparams(1),
        name="ffn_down",
    )(hm, x1, w_dnb, ln_g, ln_b)


def _sample_qkv_kernel(xb_ref, wq_ref, wk_ref, wv_ref, q_ref, kvn_ref, *, nb, n_new):
    g = pl.program_id(0)
    xb = xb_ref[...]
    res = [_dot(xb, w[...]) for w in (wq_ref, wk_ref, wv_ref)]
    q_stride = n_new * N_GROUPS * HEADS
    kv_stride = n_new * KV_ROWS
    for t in range(n_new):
        for h in range(HEADS):
            piece = lambda r: r[t * nb:(t + 1) * nb, h * LANES:(h + 1) * LANES]
            q_ref[pl.ds(t * N_GROUPS * HEADS + g * HEADS + h, nb, stride=q_stride), :] = piece(res[0])
            for a in (1, 2):
                start = g * nb * kv_stride + t * KV_ROWS + (a - 1) * HEADS + h
                kvn_ref[pl.ds(start, nb, stride=kv_stride), :] = piece(res[a])


def _sample_qkv(xb, w_inb, *, nb, n_new):
    ms = nb * n_new
    wspec = lambda blk: pl.BlockSpec((D_MODEL, COL_TILE), lambda g: (0, blk + g))
    q_rows = ms * N_GROUPS * HEADS
    kv_rows = N_GROUPS * ms * KV_ROWS
    q, kvn = pl.pallas_call(
        functools.partial(_sample_qkv_kernel, nb=nb, n_new=n_new),
        grid=(N_GROUPS,),
        in_specs=[pl.BlockSpec((ms, D_MODEL), lambda g: (0, 0)),
                  wspec(_BLK_Q), wspec(_BLK_K), wspec(_BLK_V)],
        out_specs=[pl.BlockSpec((q_rows, LANES), lambda g: (0, 0)),
                   pl.BlockSpec((kv_rows, LANES), lambda g: (0, 0))],
        out_shape=[jax.ShapeDtypeStruct((q_rows, LANES), F32),
                   jax.ShapeDtypeStruct((kv_rows, LANES), F32)],
        compiler_params=_cparams(1),
        name="sample_qkv",
    )(xb, w_inb, w_inb, w_inb)
    return (q.reshape(nb, n_new, N_GROUPS, HEADS, HEAD_DIM),
            kvn.reshape(N_GROUPS, nb, n_new, 2, HEADS, HEAD_DIM))


def _sample_attn_kernel(slope_ref, q_ref, c0_ref, c1_ref, c2_ref, kvn_ref, o_ref, *, n_new):
    c_refs = (c0_ref, c1_ref, c2_ref)
    m_idx = lax.broadcasted_iota(jnp.int32, (BAND, 1, 1), 0)
    n_idx = lax.broadcasted_iota(jnp.int32, (n_new, 1, 1), 0)
    for t in range(n_new):
        scores, values = [], []
        for g, (_, dil) in enumerate(GROUPS):
            slope = slope_ref[g][None, :, 0:1]
            q = q_ref[0, t, g][None]
            res = 0 if dil == 1 else t
            kc = c_refs[g][0, :, res, 0]
            vc = c_refs[g][0, :, res, 1]
            jc = (BAND + t - m_idx) if dil == 1 else (BAND - m_idx)
            sc = jnp.sum(kc * q, axis=-1, keepdims=True) * ATT_SCALE - slope * (dil * jc).astype(F32)
            if dil == 1:
                sc = jnp.where(jc <= BAND, sc, -jnp.inf)
                kn, vn = kvn_ref[g, 0, :, 0], kvn_ref[g, 0, :, 1]
                jn = t - n_idx
                sn = jnp.sum(kn * q, axis=-1, keepdims=True) * ATT_SCALE - slope * jn.astype(F32)
                sn = jnp.where(jn >= 0, sn, -jnp.inf)
            else:
                kn, vn = kvn_ref[g, 0, t:t + 1, 0], kvn_ref[g, 0, t:t + 1, 1]
                sn = jnp.sum(kn * q, axis=-1, keepdims=True) * ATT_SCALE
            scores += [sc, sn]
            values += [vc, vn]
        mx = functools.reduce(jnp.maximum, [jnp.max(s, axis=0, keepdims=True) for s in scores])
        ps = [jnp.exp(s - mx) for s in scores]
        den = functools.reduce(jnp.add, [jnp.sum(p, axis=0, keepdims=True) for p in ps])
        acc = functools.reduce(jnp.add, [jnp.sum(p * v, axis=0, keepdims=True) for p, v in zip(ps, values)])
        o_ref[0, t] = (acc / den)[0]


def _sample_attention(slopes, q, caches, kvn):
    nb, n_new = q.shape[0], q.shape[1]
    c_args, c_specs = [], []
    for (win, dil), c in zip(GROUPS, caches):
        res = min(dil, n_new)
        c_args.append(c.reshape(nb, win // dil, dil, 2, HEADS, HEAD_DIM))
        c_specs.append(pl.BlockSpec((1, win // dil, res, 2, HEADS, HEAD_DIM), lambda b: (b, 0, 0, 0, 0, 0)))
    return pl.pallas_call(
        functools.partial(_sample_attn_kernel, n_new=n_new),
        grid=(nb,),
        in_specs=[pl.BlockSpec(slopes.shape, lambda b: (0, 0, 0)),
                  pl.BlockSpec((1, n_new, N_GROUPS, HEADS, HEAD_DIM), lambda b: (b, 0, 0, 0, 0))]
        + c_specs
        + [pl.BlockSpec((N_GROUPS, 1, n_new, 2, HEADS, HEAD_DIM), lambda b: (0, b, 0, 0, 0, 0))],
        out_specs=pl.BlockSpec((1, n_new, HEADS, HEAD_DIM), lambda b: (b, 0, 0, 0)),
        out_shape=jax.ShapeDtypeStruct((nb, n_new, HEADS, HEAD_DIM), F32),
        compiler_params=_cparams(1),
        name="sample_attn",
    )(slopes, q, *c_args, kvn)


def _shift_cache_kernel(c_ref, nxt_ref, new_ref, o_ref, *, rows, drop):
    i = pl.program_id(1)
    last = pl.num_programs(1) - 1
    o_ref[0, 0:rows - drop, :] = c_ref[0, drop:rows, :]

    @pl.when(i < last)
    def _():
        o_ref[0, rows - drop:rows, :] = nxt_ref[0]

    @pl.when(i == last)
    def _():
        o_ref[0, rows - drop:rows, :] = new_ref[0, 0]


def _shift_cache(cache, kvn, g, *, positions):
    nb, length = cache.shape[0], cache.shape[1]
    n_new = kvn.shape[2]
    rows, drop = positions * KV_ROWS, n_new * KV_ROWS
    nblk = length // positions
    per = rows // drop
    out = pl.pallas_call(
        functools.partial(_shift_cache_kernel, rows=rows, drop=drop),
        grid=(nb, nblk),
        in_specs=[
            pl.BlockSpec((1, rows, LANES), lambda b, i: (b, i, 0)),
            pl.BlockSpec((1, drop, LANES), lambda b, i: (b, jnp.minimum(i + 1, nblk - 1) * per, 0)),
            pl.BlockSpec((1, 1, drop, LANES), lambda b, i: (g, b, 0, 0)),
        ],
        out_specs=pl.BlockSpec((1, rows, LANES), lambda b, i: (b, i, 0)),
        out_shape=jax.ShapeDtypeStruct((nb, length * KV_ROWS, LANES), cache.dtype),
        compiler_params=_cparams(2),
        name="shift_cache",
    )(cache.reshape(nb, length * KV_ROWS, LANES), cache.reshape(nb, length * KV_ROWS, LANES),
      kvn.reshape(N_GROUPS, nb, drop, LANES))
    return out.reshape(cache.shape)


def _alibi_slopes():
    h = jnp.arange(1, N_GROUPS * HEADS + 1, dtype=F32)
    return jnp.exp2(-8.0 * h / (N_GROUPS * HEADS)).reshape(N_GROUPS, HEADS)


def _prompt_bias(slopes):
    qi = jnp.arange(BAND)[:, None]
    kj = jnp.arange(2 * BAND)[None, :]
    du = BAND + qi - kj
    valid = (du >= 0) & (du <= BAND)
    tables = []
    for g, (_, dil) in enumerate(GROUPS):
        pen = slopes[g][:, None, None] * (dil * du).astype(F32)[None]
        tables.append(jnp.where(valid[None], -pen, -jnp.inf))
    return jnp.stack(tables)


def _channel_layers(ya, yb, sg, x2d, ffn_state, weights, *, tm_mix, tm_up, tm_down, chunk, shift, tps):
    w_a, w_b, w_o, w_upb, w_dnb, w_conv_ffn, ln1, ln2 = weights
    x1, x1b = _mix(ya, yb, sg, x2d, w_a, w_b, w_o, *ln1, tm=tm_mix, chunk=min(chunk, tm_mix))
    hm, cf = _ffn_up(x1b, w_upb, w_conv_ffn, ffn_state, tm=tm_up, chunk=min(2 * chunk, tm_up),
                     shift=shift, tps=tps)
    y = _ffn_down(hm, x1, w_dnb, *ln2, tm=tm_down, chunk=min(chunk, tm_down))
    return y, cf


def kernel(x_prompt, x_sample, state_conv_mix, cache_kv0, cache_kv1, cache_kv2, state_conv_ffn,
           w_in, w_conv_mix, w_branch_a, w_branch_b, w_out, ln1_g, ln1_b,
           w_up, w_conv_ffn, w_down, ln2_g, ln2_b):
    nb_p, seq, _ = x_prompt.shape
    nb_s, n_new, _ = x_sample.shape
    caches = (cache_kv0, cache_kv1, cache_kv2)

    n_proj = _BLK_GATE * COL_TILE
    w_inb = w_in[:, :n_proj].astype(BF16)
    w_gate = w_in[:, n_proj:].astype(BF16)
    weights = (w_branch_a.astype(BF16), w_branch_b.astype(BF16), w_out.astype(BF16),
               w_up.astype(BF16), w_down.astype(BF16), w_conv_ffn,
               (ln1_g.reshape(1, -1), ln1_b.reshape(1, -1)), (ln2_g.reshape(1, -1), ln2_b.reshape(1, -1)))
    slopes = _alibi_slopes()

    tm = 1024
    tps = seq // tm
    x2d = x_prompt.reshape(nb_p * seq, D_MODEL)
    xb, ya, cm_p = _proj_mix(x2d, w_inb, w_conv_mix, None, tm=tm, chunk=256, shift=1, tps=tps)
    qkv, kv_p = [], []
    for g in range(N_GROUPS):
        q, k, v, kv = _proj_qkv(xb, w_inb, g, nseq=nb_p, seqlen=seq, tm=tm, chunk=256)
        qkv += [q, k, v]
        kv_p.append(kv)
    sg = _proj_gate(xb, w_gate, tm=seq, tn=1024, chunk=256)
    yb = _attention(qkv, _prompt_bias(slopes), nseq=nb_p, seqlen=seq)
    y2d, cf_p = _channel_layers(ya, yb, sg, x2d, None, weights,
                                tm_mix=256, tm_up=seq, tm_down=512, chunk=128, shift=1, tps=1)
    y_prompt = y2d.reshape(nb_p, seq, D_MODEL)
    cm_p = cm_p[tps - 1::tps]

    ms = nb_s * n_new
    to_tm = lambda a: jnp.swapaxes(a, 0, 1).reshape(a.shape[0] * a.shape[1], a.shape[2])
    from_tm = lambda a, t: jnp.swapaxes(a.reshape(t, nb_s, a.shape[-1]), 0, 1)
    xs = to_tm(x_sample)
    xbs, yas, cm_s = _proj_mix(xs, w_inb, w_conv_mix, to_tm(state_conv_mix),
                               tm=ms, chunk=ms, shift=nb_s, tps=1)
    q_s, kvn = _sample_qkv(xbs, w_inb, nb=nb_s, n_new=n_new)
    sgs = _proj_gate(xbs, w_gate, tm=ms, tn=1024, chunk=ms)
    slope_rows = jnp.broadcast_to(slopes[:, :, None], (N_GROUPS, HEADS, HEAD_DIM))
    yb_s = _sample_attention(slope_rows, q_s, caches, kvn)
    ybs = to_tm(yb_s.reshape(nb_s, n_new, GROUP_W)).astype(BF16)
    ys, cf_s = _channel_layers(yas, ybs, sgs, xs, to_tm(state_conv_ffn), weights,
                               tm_mix=ms, tm_up=ms, tm_down=ms, chunk=ms, shift=nb_s, tps=1)
    y_sample = from_tm(ys, n_new)
    kv_s = [_shift_cache(c, kvn, g, positions=min(c.shape[1], 1024)) for g, c in enumerate(caches)]

    return (y_prompt, y_sample, cm_p, kv_p[0], kv_p[1], kv_p[2], cf_p,
            from_tm(cm_s[0], 2), kv_s[0], kv_s[1], kv_s[2], from_tm(cf_s[0], 2))
```

```python
import functools

import jax
import jax.numpy as jnp
from jax import lax
from jax.experimental import pallas as pl
from jax.experimental.pallas import tpu as pltpu

F32 = jnp.float32
BF16 = jnp.bfloat16

D_MODEL = 2048
HEAD_DIM = 128
HEADS = 4
GROUPS = ((128, 1), (512, 4), (2048, 16))
N_GROUPS = 3
GROUP_W = HEADS * HEAD_DIM
KV_ROWS = 2 * HEADS
BAND = 128
ATT_SCALE = HEAD_DIM ** -0.5
D_FF = 5632
LN_EPS = 1e-5
ALPHA = 2.0 ** 0.25

SUBLANES = 8
LANES = 128
COL_TILE = 512
DEINT_STEP = 4
PROJ_CHUNK = 256
V7X_VMEM_LIMIT = 58 * 1024 * 1024

_BLK_H, _BLK_BG, _BLK_CG = 0, 4, 8
_BLK_Q, _BLK_K, _BLK_V = 12, 15, 18
_BLK_GATE = 21


def _cparams(n_axes, vmem=V7X_VMEM_LIMIT):
    return pltpu.CompilerParams(dimension_semantics=("arbitrary",) * n_axes, vmem_limit_bytes=vmem)


def _dot(a, b):
    return jnp.dot(a, b, preferred_element_type=F32)


def _sigmoid(z):
    return 1.0 / (1.0 + jnp.exp(-z))


def _layernorm(z, g, b):
    mu = jnp.mean(z, axis=-1, keepdims=True)
    zc = z - mu
    var = jnp.mean(zc * zc, axis=-1, keepdims=True)
    return zc * lax.rsqrt(var + LN_EPS) * g + b


def _conv_base(shift):
    return -(-2 * shift // SUBLANES) * SUBLANES


def _load_prev(buf, prev_ref, carry_ref, i, tps, shift):
    p = 2 * shift
    base = _conv_base(shift)
    if prev_ref is not None:
        buf[base - p:base, :] = prev_ref[...]
    else:
        @pl.when(i % tps == 0)
        def _():
            buf[base - p:base, :] = jnp.zeros((p, buf.shape[1]), F32)

        @pl.when(i % tps != 0)
        def _():
            buf[base - p:base, :] = carry_ref[...]


def _conv_chunk(buf, u, w, r0, rows, shift):
    base = _conv_base(shift)
    buf[base + r0:base + r0 + rows, :] = u
    y = w[0:1, :] * buf[base - 2 * shift + r0:base - 2 * shift + r0 + rows, :]
    y = y + w[1:2, :] * buf[base - shift + r0:base - shift + r0 + rows, :]
    return y + w[2:3, :] * u


def _proj_mix_kernel(*refs, tm, chunk, shift, tps, has_prev):
    if has_prev:
        x_ref, wh_ref, wb_ref, wc_ref, wconv_ref, prev_ref, xb_ref, ya_ref, cm_ref, ubuf = refs
        carry_ref = None
    else:
        x_ref, wh_ref, wb_ref, wc_ref, wconv_ref, xb_ref, ya_ref, cm_ref, ubuf, carry_ref = refs
        prev_ref = None
    i, j = pl.program_id(0), pl.program_id(1)

    @pl.when(j == 0)
    def _():
        xb_ref[...] = x_ref[...].astype(BF16)

    _load_prev(ubuf, prev_ref, None if carry_ref is None else carry_ref.at[j], i, tps, shift)
    w = wconv_ref[...]
    for r0 in range(0, tm, chunk):
        xb = xb_ref[r0:r0 + chunk, :]
        u = _dot(xb, wc_ref[...]) * _dot(xb, wh_ref[...])
        y = _conv_chunk(ubuf, u, w, r0, chunk, shift)
        ya_ref[r0:r0 + chunk, :] = (_dot(xb, wb_ref[...]) * y).astype(BF16)
    base = _conv_base(shift)
    last = ubuf[base + tm - 2 * shift:base + tm, :]
    cm_ref[0] = last
    if carry_ref is not None:
        carry_ref[j] = last


def _proj_mix(x2d, w_inb, w_conv, prev, *, tm, chunk, shift, tps):
    m = x2d.shape[0]
    c = w_conv.shape[1]
    tn = COL_TILE
    nj = c // tn
    has_prev = prev is not None
    p = 2 * shift
    wspec = lambda blk: pl.BlockSpec((D_MODEL, tn), lambda i, j: (0, j + blk))
    in_specs = [pl.BlockSpec((tm, D_MODEL), lambda i, j: (i, 0)),
                wspec(_BLK_H), wspec(_BLK_BG), wspec(_BLK_CG),
                pl.BlockSpec((3, tn), lambda i, j: (0, j))]
    args = [x2d, w_inb, w_inb, w_inb, w_conv]
    scratch = [pltpu.VMEM((_conv_base(shift) + tm, tn), F32)]
    if has_prev:
        in_specs.append(pl.BlockSpec((p, tn), lambda i, j: (0, j)))
        args.append(prev)
    else:
        scratch.append(pltpu.VMEM((nj, p, tn), F32))
    return pl.pallas_call(
        functools.partial(_proj_mix_kernel, tm=tm, chunk=chunk, shift=shift, tps=tps, has_prev=has_prev),
        grid=(m // tm, nj),
        in_specs=in_specs,
        out_specs=[
            pl.BlockSpec((tm, D_MODEL), lambda i, j: (i, 0)),
            pl.BlockSpec((tm, tn), lambda i, j: (i, j)),
            pl.BlockSpec((1, p, tn), lambda i, j: (i, 0, j)),
        ],
        out_shape=[
            jax.ShapeDtypeStruct((m, D_MODEL), BF16),
            jax.ShapeDtypeStruct((m, c), BF16),
            jax.ShapeDtypeStruct((m // tm, p, c), F32),
        ],
        scratch_shapes=scratch,
        compiler_params=_cparams(2),
        name="proj_mix",
    )(*args)


def _deinterleave(slab, slab2, idx, chunk, dil):
    sub = chunk // dil
    if dil == DEINT_STEP:
        for rr in range(dil):
            yield rr, slab[idx, pl.ds(rr, sub, stride=dil), :]
        return
    quarter = chunk // DEINT_STEP
    for lo in range(DEINT_STEP):
        slab2[idx, lo * quarter:(lo + 1) * quarter, :] = slab[idx, pl.ds(lo, quarter, stride=DEINT_STEP), :]
    for lo in range(DEINT_STEP):
        for hi in range(dil // DEINT_STEP):
            yield hi * DEINT_STEP + lo, slab2[idx, pl.ds(lo * quarter + hi, sub, stride=DEINT_STEP), :]


def _proj_qkv_kernel(xb_ref, wq_ref, wk_ref, wv_ref, q_ref, k_ref, v_ref, kv_ref, slab, slab2,
                     *, tm, chunk, dil, keep):
    outs = (q_ref, k_ref, v_ref)
    sub = chunk // dil
    for r0 in range(0, tm, chunk):
        xb = xb_ref[r0:r0 + chunk, :]
        res = [_dot(xb, w[...]) for w in (wq_ref, wk_ref, wv_ref)]
        if dil == 1:
            for o_ref, r in zip(outs, res):
                o_ref[0, 0, r0:r0 + chunk, :] = r.astype(BF16)
        else:
            for a, o_ref in enumerate(outs):
                for s in range(HEADS):
                    idx = a * HEADS + s
                    slab[idx] = res[a][:, s * LANES:(s + 1) * LANES]
                    for rr, rows in _deinterleave(slab, slab2, idx, chunk, dil):
                        o_ref[0, rr, r0 // dil:r0 // dil + sub, s * LANES:(s + 1) * LANES] = rows.astype(BF16)
        lo = max(r0, tm - keep)
        if lo < r0 + chunk:
            n = r0 + chunk - lo
            for a in (1, 2):
                for h in range(HEADS):
                    start = (lo - (tm - keep)) * KV_ROWS + (a - 1) * HEADS + h
                    kv_ref[pl.ds(start, n, stride=KV_ROWS), :] = (
                        res[a][lo - r0:chunk, h * LANES:(h + 1) * LANES])


def _proj_qkv(xb, w_inb, g, *, nseq, seqlen, tm, chunk):
    win, dil = GROUPS[g]
    keep_seq = min(win, seqlen)
    keep = min(keep_seq, tm)
    tps = seqlen // tm
    n_u = seqlen // dil
    wspec = lambda blk: pl.BlockSpec((D_MODEL, COL_TILE), lambda i: (0, blk + g))
    ospec = pl.BlockSpec((1, dil, tm // dil, GROUP_W), lambda i: (i // tps, 0, i % tps, 0))
    oshape = jax.ShapeDtypeStruct((nseq, dil, n_u, GROUP_W), BF16)
    kv_index = (lambda i: (i, 0)) if keep_seq == seqlen else (lambda i: (i // tps, 0))
    q, k, v, kv = pl.pallas_call(
        functools.partial(_proj_qkv_kernel, tm=tm, chunk=chunk, dil=dil, keep=keep),
        grid=(nseq * tps,),
        in_specs=[pl.BlockSpec((tm, D_MODEL), lambda i: (i, 0)),
                  wspec(_BLK_Q), wspec(_BLK_K), wspec(_BLK_V)],
        out_specs=[ospec, ospec, ospec, pl.BlockSpec((keep * KV_ROWS, LANES), kv_index)],
        out_shape=[oshape, oshape, oshape,
                   jax.ShapeDtypeStruct((nseq * keep_seq * KV_ROWS, LANES), F32)],
        scratch_shapes=[pltpu.VMEM((3 * HEADS, chunk, LANES), F32)] * 2,
        compiler_params=_cparams(1),
        name=f"proj_qkv_dil{dil}",
    )(xb, w_inb, w_inb, w_inb)
    to_seq = lambda a: a.reshape(nseq, seqlen, GROUP_W)
    return to_seq(q), to_seq(k), to_seq(v), kv.reshape(nseq, keep_seq, 2, HEADS, HEAD_DIM)


def _proj_gate_kernel(*refs, tm, chunk, n_caches, halves, drop):
    xb_ref, w_ref = refs[0:2]
    cache_refs = refs[2:2 + 2 * n_caches]
    new_ref = refs[2 + 2 * n_caches] if n_caches else None
    o_ref = refs[2 + 2 * n_caches + (1 if n_caches else 0)]
    shifted_refs = refs[len(refs) - n_caches:] if n_caches else ()
    for r0 in range(0, tm, chunk):
        z = _dot(xb_ref[r0:r0 + chunk, :], w_ref[...])
        o_ref[r0:r0 + chunk, :] = _sigmoid(z).astype(BF16)
    if n_caches:
        step = pl.program_id(0) * pl.num_programs(1) + pl.program_id(1)
        is_last = step % halves == halves - 1
        for g in range(n_caches):
            c_ref, nxt_ref, s_ref = cache_refs[2 * g], cache_refs[2 * g + 1], shifted_refs[g]
            rows = c_ref.shape[1]
            s_ref[0, 0:rows - drop, :] = c_ref[0, drop:rows, :]

            @pl.when(jnp.logical_not(is_last))
            def _(nxt_ref=nxt_ref, s_ref=s_ref, rows=rows):
                s_ref[0, rows - drop:rows, :] = nxt_ref[0]

            @pl.when(is_last)
            def _(s_ref=s_ref, rows=rows, g=g):
                s_ref[0, rows - drop:rows, :] = new_ref[g, 0]


def _proj_gate(xb, w_gate, *, tm, tn, chunk, caches=(), kvn=None):
    m = xb.shape[0]
    n = w_gate.shape[1]
    n_i = m // tm
    steps = (n // tn) * n_i
    in_specs = [pl.BlockSpec((tm, D_MODEL), lambda c, i: (i, 0)),
                pl.BlockSpec((D_MODEL, tn), lambda c, i: (0, c))]
    out_specs = [pl.BlockSpec((tm, tn), lambda c, i: (i, c))]
    out_shape = [jax.ShapeDtypeStruct((m, n), BF16)]
    args = [xb, w_gate]
    halves, drop = 1, 0
    if caches:
        nb, n_new = kvn.shape[1], kvn.shape[2]
        halves = steps // nb
        drop = n_new * KV_ROWS
        blk = lambda c, i: ((c * n_i + i) // halves, (c * n_i + i) % halves)
        for cache in caches:
            rows = cache.shape[1] * KV_ROWS // halves
            flat = cache.reshape(nb, cache.shape[1] * KV_ROWS, LANES)
            per = rows // drop
            in_specs += [
                pl.BlockSpec((1, rows, LANES), lambda c, i: (*blk(c, i), 0)),
                pl.BlockSpec((1, drop, LANES), lambda c, i, per=per: (
                    blk(c, i)[0], jnp.minimum(blk(c, i)[1] + 1, halves - 1) * per, 0)),
            ]
            args += [flat, flat]
            out_specs.append(pl.BlockSpec((1, rows, LANES), lambda c, i: (*blk(c, i), 0)))
            out_shape.append(jax.ShapeDtypeStruct(flat.shape, cache.dtype))
        in_specs.append(pl.BlockSpec((len(caches), 1, drop, LANES), lambda c, i: (0, blk(c, i)[0], 0, 0)))
        args.append(kvn.reshape(len(caches), nb, drop, LANES))
    outs = pl.pallas_call(
        functools.partial(_proj_gate_kernel, tm=tm, chunk=chunk, n_caches=len(caches), halves=halves, drop=drop),
        grid=(n // tn, n_i),
        in_specs=in_specs,
        out_specs=out_specs,
        out_shape=out_shape,
        compiler_params=_cparams(2),
        name="proj_gate",
    )(*args)
    return outs[0], [o.reshape(c.shape) for o, c in zip(outs[1:], caches)]


ATTN_GROUP_ORDER = (2, 1, 0)
ATTN_UNROLL = 4


def _attn_block(qkv_refs, bias_ref, yb_ref, state, g, row0, pos0, first):
    q_ref, k_ref, v_ref = qkv_refs
    acc, m_s, l_s = state
    dil = GROUPS[g][1]
    opens, closes = g == ATTN_GROUP_ORDER[0], g == ATTN_GROUP_ORDER[-1]
    rows = pl.ds(pl.multiple_of(pos0, BAND), BAND) if dil == 1 else pl.ds(pos0, BAND, stride=dil)
    qrows = pl.ds(row0, BAND)
    krows = qrows if first else pl.ds(row0 - BAND, 2 * BAND)
    heads = lambda ref, r: jnp.stack([ref[0, r, h * HEAD_DIM:(h + 1) * HEAD_DIM] for h in range(HEADS)])
    q, kw, vw = heads(q_ref, qrows), heads(k_ref, krows), heads(v_ref, krows)
    bias = bias_ref[g, :, :, BAND:] if first else bias_ref[g]
    s = jnp.einsum("hqd,hkd->hqk", q, kw, preferred_element_type=F32) * ATT_SCALE + bias
    mx = jnp.max(s, axis=-1, keepdims=True)
    p = jnp.exp(s - mx)
    num = jnp.einsum("hqk,hkd->hqd", p.astype(BF16), vw, preferred_element_type=F32)
    wide = (HEADS, BAND, HEAD_DIM)
    mx = jnp.broadcast_to(mx, wide)
    den = jnp.broadcast_to(jnp.sum(p, axis=-1, keepdims=True), wide)
    if not opens:
        m_old = m_s[:, rows, :]
        m_new = jnp.maximum(m_old, mx)
        w_old, w_new = jnp.exp(m_old - m_new), jnp.exp(mx - m_new)
        num = acc[:, rows, :] * w_old + num * w_new
        den = l_s[:, rows, :] * w_old + den * w_new
        mx = m_new
    if closes:
        out = (num / den).astype(BF16)
        for h in range(HEADS):
            yb_ref[rows, h * HEAD_DIM:(h + 1) * HEAD_DIM] = out[h]
    else:
        acc[:, rows, :] = num
        m_s[:, rows, :] = mx
        l_s[:, rows, :] = den


def _attn_kernel(*refs, seqlen):
    qkv = refs[0:9]
    bias_ref, yb_ref = refs[9:11]
    state = refs[11:14]
    for g in ATTN_GROUP_ORDER:
        dil = GROUPS[g][1]
        n_u = seqlen // dil
        nblk = n_u // BAND
        block = functools.partial(_attn_block, qkv[3 * g:3 * g + 3], bias_ref, yb_ref, state, g)

        def per_residue(r, carry, block=block, n_u=n_u, nblk=nblk, dil=dil):
            base = pl.multiple_of(r * n_u, BAND)
            block(base, r, True)

            def later(qb, c):
                block(pl.multiple_of(base + qb * BAND, BAND), qb * (BAND * dil) + r, False)
                return c

            if nblk <= ATTN_UNROLL:
                for qb in range(1, nblk):
                    later(qb, carry)
                return carry
            return lax.fori_loop(1, nblk, later, carry, unroll=ATTN_UNROLL // 2)

        lax.fori_loop(0, dil, per_residue, 0, unroll=ATTN_UNROLL if nblk == 1 else 1)


def _attention(qkv, bias, *, nseq, seqlen):
    spec = pl.BlockSpec((1, seqlen, GROUP_W), lambda b: (b, 0, 0))
    return pl.pallas_call(
        functools.partial(_attn_kernel, seqlen=seqlen),
        grid=(nseq,),
        in_specs=[spec] * 9 + [pl.BlockSpec(bias.shape, lambda b: (0, 0, 0, 0))],
        out_specs=pl.BlockSpec((seqlen, GROUP_W), lambda b: (b, 0)),
        out_shape=jax.ShapeDtypeStruct((nseq * seqlen, GROUP_W), BF16),
        scratch_shapes=[pltpu.VMEM((HEADS, seqlen, HEAD_DIM), F32)] * 3,
        compiler_params=_cparams(1),
        name="attention",
    )(*qkv, bias)


def _mix_kernel(ya_ref, yb_ref, ga_ref, gb_ref, x_ref, wa_ref, wb_ref, wo_ref, g_ref, b_ref,
                x1_ref, x1b_ref, *, tm, chunk):
    for r0 in range(0, tm, chunk):
        rows = slice(r0, r0 + chunk)
        a = _dot(ya_ref[rows, :], wa_ref[...])
        b = _dot(yb_ref[rows, :], wb_ref[...])
        mixed = ga_ref[rows, :].astype(F32) * a + gb_ref[rows, :].astype(F32) * b
        z = ALPHA * x_ref[rows, :] + _dot(mixed.astype(BF16), wo_ref[...])
        out = _layernorm(z, g_ref[...], b_ref[...])
        x1_ref[rows, :] = out
        x1b_ref[rows, :] = out.astype(BF16)


def _mix(ya, yb, sg, x2d, w_a, w_b, w_o, ln_g, ln_b, *, tm, chunk):
    m = x2d.shape[0]
    row = lambda w, c=0: pl.BlockSpec((tm, w), lambda i: (i, c))
    const = lambda shape: pl.BlockSpec(shape, lambda i: (0, 0), pipeline_mode=pl.Buffered(1))
    return pl.pallas_call(
        functools.partial(_mix_kernel, tm=tm, chunk=chunk),
        grid=(m // tm,),
        in_specs=[row(D_MODEL), row(GROUP_W), row(D_MODEL, 0), row(D_MODEL, 1), row(D_MODEL),
                  const((D_MODEL, D_MODEL)), const((GROUP_W, D_MODEL)), const((D_MODEL, D_MODEL)),
                  const((1, D_MODEL)), const((1, D_MODEL))],
        out_specs=[row(D_MODEL), row(D_MODEL)],
        out_shape=[jax.ShapeDtypeStruct((m, D_MODEL), F32), jax.ShapeDtypeStruct((m, D_MODEL), BF16)],
        compiler_params=_cparams(1),
        name="mix_out",
    )(ya, yb, sg, sg, x2d, w_a, w_b, w_o, ln_g, ln_b)


def _ffn_up_kernel(*refs, tm, chunk, shift, tps, has_prev):
    if has_prev:
        x1b_ref, wa_ref, wb_ref, wconv_ref, prev_ref, hm_ref, cf_ref, abuf = refs
        carry_ref = None
    else:
        x1b_ref, wa_ref, wb_ref, wconv_ref, hm_ref, cf_ref, abuf, carry_ref = refs
        prev_ref = None
    i = pl.program_id(1)
    _load_prev(abuf, prev_ref, carry_ref, i, tps, shift)
    w = wconv_ref[...]
    for r0 in range(0, tm, chunk):
        xb = x1b_ref[r0:r0 + chunk, :]
        ac = _conv_chunk(abuf, _dot(xb, wa_ref[...]), w, r0, chunk, shift)
        hm_ref[r0:r0 + chunk, :] = (ac * _sigmoid(ac) * _dot(xb, wb_ref[...])).astype(BF16)
    base = _conv_base(shift)
    last = abuf[base + tm - 2 * shift:base + tm, :]
    cf_ref[0] = last
    if carry_ref is not None:
        carry_ref[...] = last


def _ffn_up(x1b, w_upb, w_conv, prev, *, tm, chunk, shift, tps):
    m = x1b.shape[0]
    tf = COL_TILE
    nf = D_FF // tf
    has_prev = prev is not None
    p = 2 * shift
    in_specs = [
        pl.BlockSpec((tm, D_MODEL), lambda f, i: (i, 0)),
        pl.BlockSpec((D_MODEL, tf), lambda f, i: (0, f)),
        pl.BlockSpec((D_MODEL, tf), lambda f, i: (0, f + nf)),
        pl.BlockSpec((3, tf), lambda f, i: (0, f)),
    ]
    args = [x1b, w_upb, w_upb, w_conv]
    scratch = [pltpu.VMEM((_conv_base(shift) + tm, tf), F32)]
    if has_prev:
        in_specs.append(pl.BlockSpec((p, tf), lambda f, i: (0, f)))
        args.append(prev)
    else:
        scratch.append(pltpu.VMEM((p, tf), F32))
    return pl.pallas_call(
        functools.partial(_ffn_up_kernel, tm=tm, chunk=chunk, shift=shift, tps=tps, has_prev=has_prev),
        grid=(nf, m // tm),
        in_specs=in_specs,
        out_specs=[pl.BlockSpec((tm, tf), lambda f, i: (i, f)),
                   pl.BlockSpec((1, p, tf), lambda f, i: (i, 0, f))],
        out_shape=[jax.ShapeDtypeStruct((m, D_FF), BF16),
                   jax.ShapeDtypeStruct((m // tm, p, D_FF), F32)],
        scratch_shapes=scratch,
        compiler_params=_cparams(2),
        name="ffn_up",
    )(*args)


def _ffn_down_kernel(hm_ref, x1_ref, wd_ref, g_ref, b_ref, y_ref, *, tm, chunk):
    for r0 in range(0, tm, chunk):
        rows = slice(r0, r0 + chunk)
        z = ALPHA * x1_ref[rows, :] + _dot(hm_ref[rows, :], wd_ref[...])
        y_ref[rows, :] = _layernorm(z, g_ref[...], b_ref[...])


def _ffn_down(hm, x1, w_dnb, ln_g, ln_b, *, tm, chunk):
    m = x1.shape[0]
    const = lambda shape: pl.BlockSpec(shape, lambda i: (0, 0), pipeline_mode=pl.Buffered(1))
    return pl.pallas_call(
        functools.partial(_ffn_down_kernel, tm=tm, chunk=chunk),
        grid=(m // tm,),
        in_specs=[pl.BlockSpec((tm, D_FF), lambda i: (i, 0)),
                  pl.BlockSpec((tm, D_MODEL), lambda i: (i, 0)),
                  const((D_FF, D_MODEL)), const((1, D_MODEL)), const((1, D_MODEL))],
        out_specs=pl.BlockSpec((tm, D_MODEL), lambda i: (i, 0)),
        out_shape=jax.ShapeDtypeStruct((m, D_MODEL), F32),
        compiler_params=_cparams(1),
        name="ffn_down",
    )(hm, x1, w_dnb, ln_g, ln_b)


def _sample_qkv_kernel(xb_ref, wq_ref, wk_ref, wv_ref, q_ref, kvn_ref, *, nb, n_new):
    g = pl.program_id(0)
    xb = xb_ref[...]
    res = [_dot(xb, w[...]) for w in (wq_ref, wk_ref, wv_ref)]
    q_stride = n_new * N_GROUPS * HEADS
    kv_stride = n_new * KV_ROWS
    for t in range(n_new):
        for h in range(HEADS):
            piece = lambda r: r[t * nb:(t + 1) * nb, h * LANES:(h + 1) * LANES]
            q_ref[pl.ds(t * N_GROUPS * HEADS + g * HEADS + h, nb, stride=q_stride), :] = piece(res[0])
            for a in (1, 2):
                start = g * nb * kv_stride + t * KV_ROWS + (a - 1) * HEADS + h
                kvn_ref[pl.ds(start, nb, stride=kv_stride), :] = piece(res[a])


def _sample_qkv(xb, w_inb, *, nb, n_new):
    ms = nb * n_new
    wspec = lambda blk: pl.BlockSpec((D_MODEL, COL_TILE), lambda g: (0, blk + g))
    q_rows = ms * N_GROUPS * HEADS
    kv_rows = N_GROUPS * ms * KV_ROWS
    q, kvn = pl.pallas_call(
        functools.partial(_sample_qkv_kernel, nb=nb, n_new=n_new),
        grid=(N_GROUPS,),
        in_specs=[pl.BlockSpec((ms, D_MODEL), lambda g: (0, 0)),
                  wspec(_BLK_Q), wspec(_BLK_K), wspec(_BLK_V)],
        out_specs=[pl.BlockSpec((q_rows, LANES), lambda g: (0, 0)),
                   pl.BlockSpec((kv_rows, LANES), lambda g: (0, 0))],
        out_shape=[jax.ShapeDtypeStruct((q_rows, LANES), F32),
                   jax.ShapeDtypeStruct((kv_rows, LANES), F32)],
        compiler_params=_cparams(1),
        name="sample_qkv",
    )(xb, w_inb, w_inb, w_inb)
    return (q.reshape(nb, n_new, N_GROUPS, HEADS, HEAD_DIM),
            kvn.reshape(N_GROUPS, nb, n_new, 2, HEADS, HEAD_DIM))


def _sample_attn_kernel(slope_ref, q_ref, c0_ref, c1_ref, c2_ref, kvn_ref, o_ref, *, n_new):
    c_refs = (c0_ref, c1_ref, c2_ref)
    m_idx = lax.broadcasted_iota(jnp.int32, (BAND, 1, 1), 0)
    n_idx = lax.broadcasted_iota(jnp.int32, (n_new, 1, 1), 0)
    for t in range(n_new):
        scores, values = [], []
        for g, (_, dil) in enumerate(GROUPS):
            slope = slope_ref[g][None, :, 0:1]
            q = q_ref[0, t, g][None]
            res = 0 if dil == 1 else t
            kc = c_refs[g][0, :, res, 0]
            vc = c_refs[g][0, :, res, 1]
            jc = (BAND + t - m_idx) if dil == 1 else (BAND - m_idx)
            sc = jnp.sum(kc * q, axis=-1, keepdims=True) * ATT_SCALE - slope * (dil * jc).astype(F32)
            if dil == 1:
                sc = jnp.where(jc <= BAND, sc, -jnp.inf)
                kn, vn = kvn_ref[g, 0, :, 0], kvn_ref[g, 0, :, 1]
                jn = t - n_idx
                sn = jnp.sum(kn * q, axis=-1, keepdims=True) * ATT_SCALE - slope * jn.astype(F32)
                sn = jnp.where(jn >= 0, sn, -jnp.inf)
            else:
                kn, vn = kvn_ref[g, 0, t:t + 1, 0], kvn_ref[g, 0, t:t + 1, 1]
                sn = jnp.sum(kn * q, axis=-1, keepdims=True) * ATT_SCALE
            scores += [sc, sn]
            values += [vc, vn]
        mx = functools.reduce(jnp.maximum, [jnp.max(s, axis=0, keepdims=True) for s in scores])
        ps = [jnp.exp(s - mx) for s in scores]
        den = functools.reduce(jnp.add, [jnp.sum(p, axis=0, keepdims=True) for p in ps])
        acc = functools.reduce(jnp.add, [jnp.sum(p * v, axis=0, keepdims=True) for p, v in zip(ps, values)])
        o_ref[0, t] = (acc / den)[0]


def _sample_attention(slopes, q, caches, kvn):
    nb, n_new = q.shape[0], q.shape[1]
    c_args, c_specs = [], []
    for (win, dil), c in zip(GROUPS, caches):
        res = min(dil, n_new)
        c_args.append(c.reshape(nb, win // dil, dil, 2, HEADS, HEAD_DIM))
        c_specs.append(pl.BlockSpec((1, win // dil, res, 2, HEADS, HEAD_DIM), lambda b: (b, 0, 0, 0, 0, 0)))
    return pl.pallas_call(
        functools.partial(_sample_attn_kernel, n_new=n_new),
        grid=(nb,),
        in_specs=[pl.BlockSpec(slopes.shape, lambda b: (0, 0, 0)),
                  pl.BlockSpec((1, n_new, N_GROUPS, HEADS, HEAD_DIM), lambda b: (b, 0, 0, 0, 0))]
        + c_specs
        + [pl.BlockSpec((N_GROUPS, 1, n_new, 2, HEADS, HEAD_DIM), lambda b: (0, b, 0, 0, 0, 0))],
        out_specs=pl.BlockSpec((1, n_new, HEADS, HEAD_DIM), lambda b: (b, 0, 0, 0)),
        out_shape=jax.ShapeDtypeStruct((nb, n_new, HEADS, HEAD_DIM), F32),
        compiler_params=_cparams(1),
        name="sample_attn",
    )(slopes, q, *c_args, kvn)


def _alibi_slopes():
    h = jnp.arange(1, N_GROUPS * HEADS + 1, dtype=F32)
    return jnp.exp2(-8.0 * h / (N_GROUPS * HEADS)).reshape(N_GROUPS, HEADS)


def _prompt_bias(slopes):
    qi = jnp.arange(BAND)[:, None]
    kj = jnp.arange(2 * BAND)[None, :]
    du = BAND + qi - kj
    valid = (du >= 0) & (du <= BAND)
    tables = []
    for g, (_, dil) in enumerate(GROUPS):
        pen = slopes[g][:, None, None] * (dil * du).astype(F32)[None]
        tables.append(jnp.where(valid[None], -pen, -jnp.inf))
    return jnp.stack(tables)


def _channel_layers(ya, yb, sg, x2d, ffn_state, weights, *, tiles, shift, tps):
    w_a, w_b, w_o, w_upb, w_dnb, w_conv_ffn, ln1, ln2 = weights
    (tm_mix, c_mix), (tm_up, c_up), (tm_down, c_down) = tiles
    x1, x1b = _mix(ya, yb, sg, x2d, w_a, w_b, w_o, *ln1, tm=tm_mix, chunk=c_mix)
    hm, cf = _ffn_up(x1b, w_upb, w_conv_ffn, ffn_state, tm=tm_up, chunk=c_up, shift=shift, tps=tps)
    y = _ffn_down(hm, x1, w_dnb, *ln2, tm=tm_down, chunk=c_down)
    return y, cf


def kernel(x_prompt, x_sample, state_conv_mix, cache_kv0, cache_kv1, cache_kv2, state_conv_ffn,
           w_in, w_conv_mix, w_branch_a, w_branch_b, w_out, ln1_g, ln1_b,
           w_up, w_conv_ffn, w_down, ln2_g, ln2_b):
    nb_p, seq, _ = x_prompt.shape
    nb_s, n_new, _ = x_sample.shape
    caches = (cache_kv0, cache_kv1, cache_kv2)

    n_proj = _BLK_GATE * COL_TILE
    w_inb = w_in[:, :n_proj].astype(BF16)
    w_gate = w_in[:, n_proj:].astype(BF16)
    weights = (w_branch_a.astype(BF16), w_branch_b.astype(BF16), w_out.astype(BF16),
               w_up.astype(BF16), w_down.astype(BF16), w_conv_ffn,
               (ln1_g.reshape(1, -1), ln1_b.reshape(1, -1)), (ln2_g.reshape(1, -1), ln2_b.reshape(1, -1)))
    slopes = _alibi_slopes()

    ms = nb_s * n_new
    to_tm = lambda a: jnp.swapaxes(a, 0, 1).reshape(a.shape[0] * a.shape[1], a.shape[2])
    from_tm = lambda a, t: jnp.swapaxes(a.reshape(t, nb_s, a.shape[-1]), 0, 1)
    xs = to_tm(x_sample)
    xbs, yas, cm_s = _proj_mix(xs, w_inb, w_conv_mix, to_tm(state_conv_mix),
                               tm=ms, chunk=ms, shift=nb_s, tps=1)
    q_s, kvn = _sample_qkv(xbs, w_inb, nb=nb_s, n_new=n_new)
    sgs, _ = _proj_gate(xbs, w_gate, tm=ms, tn=1024, chunk=ms)

    tm = 1024
    tps = seq // tm
    x2d = x_prompt.reshape(nb_p * seq, D_MODEL)
    xb, ya, cm_p = _proj_mix(x2d, w_inb, w_conv_mix, None, tm=tm, chunk=PROJ_CHUNK, shift=1, tps=tps)
    qkv, kv_p = [], []
    for g in range(N_GROUPS):
        q, k, v, kv = _proj_qkv(xb, w_inb, g, nseq=nb_p, seqlen=seq, tm=tm, chunk=PROJ_CHUNK)
        qkv += [q, k, v]
        kv_p.append(kv)
    sg, kv_s = _proj_gate(xb, w_gate, tm=seq, tn=1024, chunk=PROJ_CHUNK, caches=caches, kvn=kvn)
    yb = _attention(qkv, _prompt_bias(slopes), nseq=nb_p, seqlen=seq)
    y2d, cf_p = _channel_layers(ya, yb, sg, x2d, None, weights,
                                tiles=((256, 128), (seq, seq // 2), (512, 128)), shift=1, tps=1)
    y_prompt = y2d.reshape(nb_p, seq, D_MODEL)
    cm_p = cm_p[tps - 1::tps]

    slope_rows = jnp.broadcast_to(slopes[:, :, None], (N_GROUPS, HEADS, HEAD_DIM))
    yb_s = _sample_attention(slope_rows, q_s, caches, kvn)
    ybs = to_tm(yb_s.reshape(nb_s, n_new, GROUP_W)).astype(BF16)
    ys, cf_s = _channel_layers(yas, ybs, sgs, xs, to_tm(state_conv_ffn), weights,
                               tiles=((ms, ms),) * 3, shift=nb_s, tps=1)
    y_sample = from_tm(ys, n_new)

    return (y_prompt, y_sample, cm_p, kv_p[0], kv_p[1], kv_p[2], cf_p,
            from_tm(cm_s[0], 2), kv_s[0], kv_s[1], kv_s[2], from_tm(cf_s[0], 2))
```

```python
import functools

import jax
import jax.numpy as jnp
from jax import lax
from jax.experimental import pallas as pl
from jax.experimental.pallas import tpu as pltpu

F32 = jnp.float32
BF16 = jnp.bfloat16

D_MODEL = 2048
HEAD_DIM = 128
HEADS = 4
GROUPS = ((128, 1), (512, 4), (2048, 16))
N_GROUPS = 3
GROUP_W = HEADS * HEAD_DIM
KV_ROWS = 2 * HEADS
BAND = 128
ATT_SCALE = HEAD_DIM ** -0.5
D_FF = 5632
LN_EPS = 1e-5
ALPHA = 2.0 ** 0.25

SUBLANES = 8
LANES = 128
COL_TILE = 512
DEINT_STEP = 4
PROJ_CHUNK = 256
V7X_VMEM_LIMIT = 58 * 1024 * 1024

_BLK_H, _BLK_BG, _BLK_CG = 0, 4, 8
_BLK_Q, _BLK_K, _BLK_V = 12, 15, 18
_BLK_GATE = 21


def _cparams(n_axes, vmem=V7X_VMEM_LIMIT):
    return pltpu.CompilerParams(dimension_semantics=("arbitrary",) * n_axes, vmem_limit_bytes=vmem)


def _dot(a, b):
    return jnp.dot(a, b, preferred_element_type=F32)


def _sigmoid(z):
    return 1.0 / (1.0 + jnp.exp(-z))


def _layernorm(z, g, b):
    mu = jnp.mean(z, axis=-1, keepdims=True)
    zc = z - mu
    var = jnp.mean(zc * zc, axis=-1, keepdims=True)
    return zc * lax.rsqrt(var + LN_EPS) * g + b


def _conv_base(shift):
    return -(-2 * shift // SUBLANES) * SUBLANES


def _load_prev(buf, prev_ref, carry_ref, i, tps, shift):
    p = 2 * shift
    base = _conv_base(shift)
    if prev_ref is not None:
        buf[base - p:base, :] = prev_ref[...]
    else:
        @pl.when(i % tps == 0)
        def _():
            buf[base - p:base, :] = jnp.zeros((p, buf.shape[1]), F32)

        @pl.when(i % tps != 0)
        def _():
            buf[base - p:base, :] = carry_ref[...]


def _conv_chunk(buf, u, w, r0, rows, shift):
    base = _conv_base(shift)
    buf[base + r0:base + r0 + rows, :] = u
    y = w[0:1, :] * buf[base - 2 * shift + r0:base - 2 * shift + r0 + rows, :]
    y = y + w[1:2, :] * buf[base - shift + r0:base - shift + r0 + rows, :]
    return y + w[2:3, :] * u


def _proj_mix_kernel(*refs, tm, chunk, shift, tps, has_prev):
    if has_prev:
        x_ref, wh_ref, wb_ref, wc_ref, wconv_ref, prev_ref, xb_ref, ya_ref, cm_ref, ubuf = refs
        carry_ref = None
    else:
        x_ref, wh_ref, wb_ref, wc_ref, wconv_ref, xb_ref, ya_ref, cm_ref, ubuf, carry_ref = refs
        prev_ref = None
    i, j = pl.program_id(0), pl.program_id(1)

    @pl.when(j == 0)
    def _():
        xb_ref[...] = x_ref[...].astype(BF16)

    _load_prev(ubuf, prev_ref, None if carry_ref is None else carry_ref.at[j], i, tps, shift)
    w = wconv_ref[...]
    for r0 in range(0, tm, chunk):
        xb = xb_ref[r0:r0 + chunk, :]
        u = _dot(xb, wc_ref[...]) * _dot(xb, wh_ref[...])
        y = _conv_chunk(ubuf, u, w, r0, chunk, shift)
        ya_ref[r0:r0 + chunk, :] = (_dot(xb, wb_ref[...]) * y).astype(BF16)
    base = _conv_base(shift)
    last = ubuf[base + tm - 2 * shift:base + tm, :]
    cm_ref[0] = last
    if carry_ref is not None:
        carry_ref[j] = last


def _proj_mix(x2d, w_inb, w_conv, prev, *, tm, chunk, shift, tps):
    m = x2d.shape[0]
    c = w_conv.shape[1]
    tn = COL_TILE
    nj = c // tn
    has_prev = prev is not None
    p = 2 * shift
    wspec = lambda blk: pl.BlockSpec((D_MODEL, tn), lambda i, j: (0, j + blk))
    in_specs = [pl.BlockSpec((tm, D_MODEL), lambda i, j: (i, 0)),
                wspec(_BLK_H), wspec(_BLK_BG), wspec(_BLK_CG),
                pl.BlockSpec((3, tn), lambda i, j: (0, j))]
    args = [x2d, w_inb, w_inb, w_inb, w_conv]
    scratch = [pltpu.VMEM((_conv_base(shift) + tm, tn), F32)]
    if has_prev:
        in_specs.append(pl.BlockSpec((p, tn), lambda i, j: (0, j)))
        args.append(prev)
    else:
        scratch.append(pltpu.VMEM((nj, p, tn), F32))
    return pl.pallas_call(
        functools.partial(_proj_mix_kernel, tm=tm, chunk=chunk, shift=shift, tps=tps, has_prev=has_prev),
        grid=(m // tm, nj),
        in_specs=in_specs,
        out_specs=[
            pl.BlockSpec((tm, D_MODEL), lambda i, j: (i, 0)),
            pl.BlockSpec((tm, tn), lambda i, j: (i, j)),
            pl.BlockSpec((1, p, tn), lambda i, j: (i, 0, j)),
        ],
        out_shape=[
            jax.ShapeDtypeStruct((m, D_MODEL), BF16),
            jax.ShapeDtypeStruct((m, c), BF16),
            jax.ShapeDtypeStruct((m // tm, p, c), F32),
        ],
        scratch_shapes=scratch,
        compiler_params=_cparams(2),
        name="proj_mix",
    )(*args)


def _deinterleave(slab, slab2, idx, chunk, dil):
    sub = chunk // dil
    if dil == DEINT_STEP:
        for rr in range(dil):
            yield rr, slab[idx, pl.ds(rr, sub, stride=dil), :]
        return
    quarter = chunk // DEINT_STEP
    for lo in range(DEINT_STEP):
        slab2[idx, lo * quarter:(lo + 1) * quarter, :] = slab[idx, pl.ds(lo, quarter, stride=DEINT_STEP), :]
    for lo in range(DEINT_STEP):
        for hi in range(dil // DEINT_STEP):
            yield hi * DEINT_STEP + lo, slab2[idx, pl.ds(lo * quarter + hi, sub, stride=DEINT_STEP), :]


def _proj_qkv_kernel(xb_ref, wq_ref, wk_ref, wv_ref, q_ref, k_ref, v_ref, kv_ref, slab, slab2,
                     *, tm, chunk, dil, keep):
    outs = (q_ref, k_ref, v_ref)
    sub = chunk // dil
    for r0 in range(0, tm, chunk):
        xb = xb_ref[r0:r0 + chunk, :]
        res = [_dot(xb, w[...]) for w in (wq_ref, wk_ref, wv_ref)]
        if dil == 1:
            for o_ref, r in zip(outs, res):
                o_ref[0, 0, r0:r0 + chunk, :] = r.astype(BF16)
        else:
            for a, o_ref in enumerate(outs):
                for s in range(HEADS):
                    idx = a * HEADS + s
                    slab[idx] = res[a][:, s * LANES:(s + 1) * LANES]
                    for rr, rows in _deinterleave(slab, slab2, idx, chunk, dil):
                        o_ref[0, rr, r0 // dil:r0 // dil + sub, s * LANES:(s + 1) * LANES] = rows.astype(BF16)
        lo = max(r0, tm - keep)
        if lo < r0 + chunk:
            n = r0 + chunk - lo
            for a in (1, 2):
                for h in range(HEADS):
                    start = (lo - (tm - keep)) * KV_ROWS + (a - 1) * HEADS + h
                    kv_ref[pl.ds(start, n, stride=KV_ROWS), :] = (
                        res[a][lo - r0:chunk, h * LANES:(h + 1) * LANES])


def _proj_qkv(xb, w_inb, g, *, nseq, seqlen, tm, chunk):
    win, dil = GROUPS[g]
    keep_seq = min(win, seqlen)
    keep = min(keep_seq, tm)
    tps = seqlen // tm
    n_u = seqlen // dil
    wspec = lambda blk: pl.BlockSpec((D_MODEL, COL_TILE), lambda i: (0, blk + g))
    ospec = pl.BlockSpec((1, dil, tm // dil, GROUP_W), lambda i: (i // tps, 0, i % tps, 0))
    oshape = jax.ShapeDtypeStruct((nseq, dil, n_u, GROUP_W), BF16)
    kv_index = (lambda i: (i, 0)) if keep_seq == seqlen else (lambda i: (i // tps, 0))
    q, k, v, kv = pl.pallas_call(
        functools.partial(_proj_qkv_kernel, tm=tm, chunk=chunk, dil=dil, keep=keep),
        grid=(nseq * tps,),
        in_specs=[pl.BlockSpec((tm, D_MODEL), lambda i: (i, 0)),
                  wspec(_BLK_Q), wspec(_BLK_K), wspec(_BLK_V)],
        out_specs=[ospec, ospec, ospec, pl.BlockSpec((keep * KV_ROWS, LANES), kv_index)],
        out_shape=[oshape, oshape, oshape,
                   jax.ShapeDtypeStruct((nseq * keep_seq * KV_ROWS, LANES), F32)],
        scratch_shapes=[pltpu.VMEM((3 * HEADS, chunk, LANES), F32)] * 2,
        compiler_params=_cparams(1),
        name=f"proj_qkv_dil{dil}",
    )(xb, w_inb, w_inb, w_inb)
    to_seq = lambda a: a.reshape(nseq, seqlen, GROUP_W)
    return to_seq(q), to_seq(k), to_seq(v), kv.reshape(nseq, keep_seq, 2, HEADS, HEAD_DIM)


def _proj_gate_kernel(*refs, tm, chunk, n_caches, halves, drop):
    xb_ref, w_ref = refs[0:2]
    cache_refs = refs[2:2 + 2 * n_caches]
    new_ref = refs[2 + 2 * n_caches] if n_caches else None
    o_ref = refs[2 + 2 * n_caches + (1 if n_caches else 0)]
    shifted_refs = refs[len(refs) - n_caches:] if n_caches else ()
    for r0 in range(0, tm, chunk):
        z = _dot(xb_ref[r0:r0 + chunk, :], w_ref[...])
        o_ref[r0:r0 + chunk, :] = _sigmoid(z).astype(BF16)
    if n_caches:
        step = pl.program_id(0) * pl.num_programs(1) + pl.program_id(1)
        is_last = step % halves == halves - 1
        for g in range(n_caches):
            c_ref, nxt_ref, s_ref = cache_refs[2 * g], cache_refs[2 * g + 1], shifted_refs[g]
            rows = c_ref.shape[1]
            s_ref[0, 0:rows - drop, :] = c_ref[0, drop:rows, :]

            @pl.when(jnp.logical_not(is_last))
            def _(nxt_ref=nxt_ref, s_ref=s_ref, rows=rows):
                s_ref[0, rows - drop:rows, :] = nxt_ref[0]

            @pl.when(is_last)
            def _(s_ref=s_ref, rows=rows, g=g):
                s_ref[0, rows - drop:rows, :] = new_ref[g, 0]


def _proj_gate(xb, w_gate, *, tm, tn, chunk, caches=(), kvn=None):
    m = xb.shape[0]
    n = w_gate.shape[1]
    n_i = m // tm
    steps = (n // tn) * n_i
    in_specs = [pl.BlockSpec((tm, D_MODEL), lambda c, i: (i, 0)),
                pl.BlockSpec((D_MODEL, tn), lambda c, i: (0, c))]
    out_specs = [pl.BlockSpec((tm, tn), lambda c, i: (i, c))]
    out_shape = [jax.ShapeDtypeStruct((m, n), BF16)]
    args = [xb, w_gate]
    halves, drop = 1, 0
    if caches:
        nb, n_new = kvn.shape[1], kvn.shape[2]
        halves = steps // nb
        drop = n_new * KV_ROWS
        blk = lambda c, i: ((c * n_i + i) // halves, (c * n_i + i) % halves)
        for cache in caches:
            rows = cache.shape[1] * KV_ROWS // halves
            flat = cache.reshape(nb, cache.shape[1] * KV_ROWS, LANES)
            per = rows // drop
            in_specs += [
                pl.BlockSpec((1, rows, LANES), lambda c, i: (*blk(c, i), 0)),
                pl.BlockSpec((1, drop, LANES), lambda c, i, per=per: (
                    blk(c, i)[0], jnp.minimum(blk(c, i)[1] + 1, halves - 1) * per, 0)),
            ]
            args += [flat, flat]
            out_specs.append(pl.BlockSpec((1, rows, LANES), lambda c, i: (*blk(c, i), 0)))
            out_shape.append(jax.ShapeDtypeStruct(flat.shape, cache.dtype))
        in_specs.append(pl.BlockSpec((len(caches), 1, drop, LANES), lambda c, i: (0, blk(c, i)[0], 0, 0)))
        args.append(kvn.reshape(len(caches), nb, drop, LANES))
    outs = pl.pallas_call(
        functools.partial(_proj_gate_kernel, tm=tm, chunk=chunk, n_caches=len(caches), halves=halves, drop=drop),
        grid=(n // tn, n_i),
        in_specs=in_specs,
        out_specs=out_specs,
        out_shape=out_shape,
        compiler_params=_cparams(2),
        name="proj_gate",
    )(*args)
    return outs[0], [o.reshape(c.shape) for o, c in zip(outs[1:], caches)]


ATTN_GROUP_ORDER = (2, 1, 0)
ATTN_UNROLL = 4


def _attn_block(qkv_refs, bias_ref, yb_ref, state, g, row0, pos0, first):
    q_ref, k_ref, v_ref = qkv_refs
    acc, m_s, l_s = state
    dil = GROUPS[g][1]
    opens, closes = g == ATTN_GROUP_ORDER[0], g == ATTN_GROUP_ORDER[-1]
    rows = pl.ds(pl.multiple_of(pos0, BAND), BAND) if dil == 1 else pl.ds(pos0, BAND, stride=dil)
    qrows = pl.ds(row0, BAND)
    krows = qrows if first else pl.ds(row0 - BAND, 2 * BAND)
    heads = lambda ref, r: jnp.stack([ref[0, r, h * HEAD_DIM:(h + 1) * HEAD_DIM] for h in range(HEADS)])
    q, kw, vw = heads(q_ref, qrows), heads(k_ref, krows), heads(v_ref, krows)
    bias = bias_ref[g, :, :, BAND:] if first else bias_ref[g]
    s = jnp.einsum("hqd,hkd->hqk", q, kw, preferred_element_type=F32) * ATT_SCALE + bias
    mx = jnp.max(s, axis=-1, keepdims=True)
    p = jnp.exp(s - mx)
    num = jnp.einsum("hqk,hkd->hqd", p.astype(BF16), vw, preferred_element_type=F32)
    wide = (HEADS, BAND, HEAD_DIM)
    mx = jnp.broadcast_to(mx, wide)
    den = jnp.broadcast_to(jnp.sum(p, axis=-1, keepdims=True), wide)
    if not opens:
        m_old = m_s[:, rows, :]
        m_new = jnp.maximum(m_old, mx)
        w_old, w_new = jnp.exp(m_old - m_new), jnp.exp(mx - m_new)
        num = acc[:, rows, :] * w_old + num * w_new
        den = l_s[:, rows, :] * w_old + den * w_new
        mx = m_new
    if closes:
        out = (num / den).astype(BF16)
        for h in range(HEADS):
            yb_ref[rows, h * HEAD_DIM:(h + 1) * HEAD_DIM] = out[h]
    else:
        acc[:, rows, :] = num
        m_s[:, rows, :] = mx
        l_s[:, rows, :] = den


def _attn_kernel(*refs, seqlen):
    qkv = refs[0:9]
    bias_ref, yb_ref = refs[9:11]
    state = refs[11:14]
    for g in ATTN_GROUP_ORDER:
        dil = GROUPS[g][1]
        n_u = seqlen // dil
        nblk = n_u // BAND
        block = functools.partial(_attn_block, qkv[3 * g:3 * g + 3], bias_ref, yb_ref, state, g)

        def per_residue(r, carry, block=block, n_u=n_u, nblk=nblk, dil=dil):
            base = pl.multiple_of(r * n_u, BAND)
            block(base, r, True)

            def later(qb, c):
                block(pl.multiple_of(base + qb * BAND, BAND), qb * (BAND * dil) + r, False)
                return c

            if nblk <= ATTN_UNROLL:
                for qb in range(1, nblk):
                    later(qb, carry)
                return carry
            return lax.fori_loop(1, nblk, later, carry, unroll=ATTN_UNROLL // 2)

        lax.fori_loop(0, dil, per_residue, 0, unroll=ATTN_UNROLL if nblk == 1 else 1)


def _attention(qkv, bias, *, nseq, seqlen):
    spec = pl.BlockSpec((1, seqlen, GROUP_W), lambda b: (b, 0, 0))
    return pl.pallas_call(
        functools.partial(_attn_kernel, seqlen=seqlen),
        grid=(nseq,),
        in_specs=[spec] * 9 + [pl.BlockSpec(bias.shape, lambda b: (0, 0, 0, 0))],
        out_specs=pl.BlockSpec((seqlen, GROUP_W), lambda b: (b, 0)),
        out_shape=jax.ShapeDtypeStruct((nseq * seqlen, GROUP_W), BF16),
        scratch_shapes=[pltpu.VMEM((HEADS, seqlen, HEAD_DIM), F32)] * 3,
        compiler_params=_cparams(1),
        name="attention",
    )(*qkv, bias)


def _mix_kernel(ya_ref, yb_ref, ga_ref, gb_ref, x_ref, wa_ref, wb_ref, wo_ref, g_ref, b_ref,
                x1_ref, x1b_ref, *, tm, chunk):
    for r0 in range(0, tm, chunk):
        rows = slice(r0, r0 + chunk)
        a = _dot(ya_ref[rows, :], wa_ref[...])
        b = _dot(yb_ref[rows, :], wb_ref[...])
        mixed = ga_ref[rows, :].astype(F32) * a + gb_ref[rows, :].astype(F32) * b
        z = ALPHA * x_ref[rows, :] + _dot(mixed.astype(BF16), wo_ref[...])
        out = _layernorm(z, g_ref[...], b_ref[...])
        x1_ref[rows, :] = out
        x1b_ref[rows, :] = out.astype(BF16)


def _mix(ya, yb, sg, x2d, w_a, w_b, w_o, ln_g, ln_b, *, tm, chunk):
    m = x2d.shape[0]
    row = lambda w, c=0: pl.BlockSpec((tm, w), lambda i: (i, c))
    const = lambda shape: pl.BlockSpec(shape, lambda i: (0, 0), pipeline_mode=pl.Buffered(1))
    return pl.pallas_call(
        functools.partial(_mix_kernel, tm=tm, chunk=chunk),
        grid=(m // tm,),
        in_specs=[row(D_MODEL), row(GROUP_W), row(D_MODEL, 0), row(D_MODEL, 1), row(D_MODEL),
                  const((D_MODEL, D_MODEL)), const((GROUP_W, D_MODEL)), const((D_MODEL, D_MODEL)),
                  const((1, D_MODEL)), const((1, D_MODEL))],
        out_specs=[row(D_MODEL), row(D_MODEL)],
        out_shape=[jax.ShapeDtypeStruct((m, D_MODEL), F32), jax.ShapeDtypeStruct((m, D_MODEL), BF16)],
        compiler_params=_cparams(1),
        name="mix_out",
    )(ya, yb, sg, sg, x2d, w_a, w_b, w_o, ln_g, ln_b)


def _ffn_up_kernel(*refs, tm, chunk, shift, tps, has_prev):
    if has_prev:
        x1b_ref, wa_ref, wb_ref, wconv_ref, prev_ref, hm_ref, cf_ref, abuf = refs
        carry_ref = None
    else:
        x1b_ref, wa_ref, wb_ref, wconv_ref, hm_ref, cf_ref, abuf, carry_ref = refs
        prev_ref = None
    i = pl.program_id(1)
    _load_prev(abuf, prev_ref, carry_ref, i, tps, shift)
    w = wconv_ref[...]
    for r0 in range(0, tm, chunk):
        xb = x1b_ref[r0:r0 + chunk, :]
        ac = _conv_chunk(abuf, _dot(xb, wa_ref[...]), w, r0, chunk, shift)
        hm_ref[r0:r0 + chunk, :] = (ac * _sigmoid(ac) * _dot(xb, wb_ref[...])).astype(BF16)
    base = _conv_base(shift)
    last = abuf[base + tm - 2 * shift:base + tm, :]
    cf_ref[0] = last
    if carry_ref is not None:
        carry_ref[...] = last


def _ffn_up(x1b, w_upb, w_conv, prev, *, tm, chunk, shift, tps):
    m = x1b.shape[0]
    tf = COL_TILE
    nf = D_FF // tf
    has_prev = prev is not None
    p = 2 * shift
    in_specs = [
        pl.BlockSpec((tm, D_MODEL), lambda f, i: (i, 0)),
        pl.BlockSpec((D_MODEL, tf), lambda f, i: (0, f)),
        pl.BlockSpec((D_MODEL, tf), lambda f, i: (0, f + nf)),
        pl.BlockSpec((3, tf), lambda f, i: (0, f)),
    ]
    args = [x1b, w_upb, w_upb, w_conv]
    scratch = [pltpu.VMEM((_conv_base(shift) + tm, tf), F32)]
    if has_prev:
        in_specs.append(pl.BlockSpec((p, tf), lambda f, i: (0, f)))
        args.append(prev)
    else:
        scratch.append(pltpu.VMEM((p, tf), F32))
    return pl.pallas_call(
        functools.partial(_ffn_up_kernel, tm=tm, chunk=chunk, shift=shift, tps=tps, has_prev=has_prev),
        grid=(nf, m // tm),
        in_specs=in_specs,
        out_specs=[pl.BlockSpec((tm, tf), lambda f, i: (i, f)),
                   pl.BlockSpec((1, p, tf), lambda f, i: (i, 0, f))],
        out_shape=[jax.ShapeDtypeStruct((m, D_FF), BF16),
                   jax.ShapeDtypeStruct((m // tm, p, D_FF), F32)],
        scratch_shapes=scratch,
        compiler_params=_cparams(2),
        name="ffn_up",
    )(*args)


def _ffn_down_kernel(hm_ref, x1_ref, wd_ref, g_ref, b_ref, y_ref, *, tm, chunk):
    for r0 in range(0, tm, chunk):
        rows = slice(r0, r0 + chunk)
        z = ALPHA * x1_ref[rows, :] + _dot(hm_ref[rows, :], wd_ref[...])
        y_ref[rows, :] = _layernorm(z, g_ref[...], b_ref[...])


def _ffn_down(hm, x1, w_dnb, ln_g, ln_b, *, tm, chunk):
    m = x1.shape[0]
    const = lambda shape: pl.BlockSpec(shape, lambda i: (0, 0), pipeline_mode=pl.Buffered(1))
    return pl.pallas_call(
        functools.partial(_ffn_down_kernel, tm=tm, chunk=chunk),
        grid=(m // tm,),
        in_specs=[pl.BlockSpec((tm, D_FF), lambda i: (i, 0)),
                  pl.BlockSpec((tm, D_MODEL), lambda i: (i, 0)),
                  const((D_FF, D_MODEL)), const((1, D_MODEL)), const((1, D_MODEL))],
        out_specs=pl.BlockSpec((tm, D_MODEL), lambda i: (i, 0)),
        out_shape=jax.ShapeDtypeStruct((m, D_MODEL), F32),
        compiler_params=_cparams(1),
        name="ffn_down",
    )(hm, x1, w_dnb, ln_g, ln_b)


def _sample_qkv_kernel(xb_ref, wq_ref, wk_ref, wv_ref, q_ref, kvn_ref, *, nb, n_new):
    g = pl.program_id(0)
    xb = xb_ref[...]
    res = [_dot(xb, w[...]) for w in (wq_ref, wk_ref, wv_ref)]
    q_stride = n_new * N_GROUPS * HEADS
    kv_stride = n_new * KV_ROWS
    for t in range(n_new):
        for h in range(HEADS):
            piece = lambda r: r[t * nb:(t + 1) * nb, h * LANES:(h + 1) * LANES]
            q_ref[pl.ds(t * N_GROUPS * HEADS + g * HEADS + h, nb, stride=q_stride), :] = piece(res[0])
            for a in (1, 2):
                start = g * nb * kv_stride + t * KV_ROWS + (a - 1) * HEADS + h
                kvn_ref[pl.ds(start, nb, stride=kv_stride), :] = piece(res[a])


def _sample_qkv(xb, w_inb, *, nb, n_new):
    ms = nb * n_new
    wspec = lambda blk: pl.BlockSpec((D_MODEL, COL_TILE), lambda g: (0, blk + g))
    q_rows = ms * N_GROUPS * HEADS
    kv_rows = N_GROUPS * ms * KV_ROWS
    q, kvn = pl.pallas_call(
        functools.partial(_sample_qkv_kernel, nb=nb, n_new=n_new),
        grid=(N_GROUPS,),
        in_specs=[pl.BlockSpec((ms, D_MODEL), lambda g: (0, 0)),
                  wspec(_BLK_Q), wspec(_BLK_K), wspec(_BLK_V)],
        out_specs=[pl.BlockSpec((q_rows, LANES), lambda g: (0, 0)),
                   pl.BlockSpec((kv_rows, LANES), lambda g: (0, 0))],
        out_shape=[jax.ShapeDtypeStruct((q_rows, LANES), F32),
                   jax.ShapeDtypeStruct((kv_rows, LANES), F32)],
        compiler_params=_cparams(1),
        name="sample_qkv",
    )(xb, w_inb, w_inb, w_inb)
    return (q.reshape(nb, n_new, N_GROUPS, HEADS, HEAD_DIM),
            kvn.reshape(N_GROUPS, nb, n_new, 2, HEADS, HEAD_DIM))


def _sample_attn_kernel(slope_ref, table_ref, q_ref, c0_ref, c1_ref, c2_ref, kvn_ref, o_ref, *, n_new):
    c_refs = (c0_ref, c1_ref, c2_ref)
    n_idx = lax.broadcasted_iota(jnp.int32, (n_new, 1, 1), 0)
    pad = jnp.zeros((SUBLANES - HEADS, HEAD_DIM), F32)
    for t in range(n_new):
        s_cache, s_new, v_cache, v_new = [], [], [], []
        for g, (_, dil) in enumerate(GROUPS):
            q = q_ref[0, t, g]
            res = 0 if dil == 1 else t
            kc = c_refs[g][0, :, res, 0].reshape(BAND * HEADS, HEAD_DIM)
            v_cache.append(c_refs[g][0, :, res, 1].reshape(BAND * HEADS, HEAD_DIM))
            sc = lax.dot_general(jnp.concatenate([q, pad], axis=0), kc, (((1,), (1,)), ((), ())),
                                 preferred_element_type=F32)
            s_cache.append(sc * ATT_SCALE + table_ref[g, t])
            if dil == 1:
                slope = slope_ref[g][None, :, 0:1]
                kn, vn = kvn_ref[g, 0, :, 0], kvn_ref[g, 0, :, 1]
                jn = t - n_idx
                sn = jnp.sum(kn * q[None], axis=-1, keepdims=True) * ATT_SCALE - slope * jn.astype(F32)
                sn = jnp.where(jn >= 0, sn, -jnp.inf)
            else:
                kn, vn = kvn_ref[g, 0, t:t + 1, 0], kvn_ref[g, 0, t:t + 1, 1]
                sn = jnp.sum(kn * q[None], axis=-1, keepdims=True) * ATT_SCALE
            s_new.append(sn)
            v_new.append(vn)
        mx = functools.reduce(jnp.maximum, [jnp.max(s, axis=-1, keepdims=True) for s in s_cache])
        mx_h = functools.reduce(jnp.maximum, [jnp.max(s, axis=0) for s in s_new] + [mx[0:HEADS]])
        mx = jnp.concatenate([mx_h, mx[HEADS:]], axis=0)
        p_cache = [jnp.exp(s - mx) for s in s_cache]
        p_new = [jnp.exp(s - mx_h[None]) for s in s_new]
        den = functools.reduce(jnp.add, [jnp.sum(p, axis=-1, keepdims=True)[0:HEADS] for p in p_cache]
                               + [jnp.sum(p, axis=0) for p in p_new])
        acc = functools.reduce(jnp.add, [_dot(p, v)[0:HEADS] for p, v in zip(p_cache, v_cache)]
                               + [jnp.sum(p * v, axis=0) for p, v in zip(p_new, v_new)])
        o_ref[0, t] = acc / den


def _sample_tables(slopes, n_new):
    m = jnp.arange(BAND)[None, :, None]
    h_key = jnp.arange(HEADS)[None, None, :]
    h_q = jnp.arange(SUBLANES)[:, None, None]
    tables = []
    for g, (_, dil) in enumerate(GROUPS):
        per_t = []
        for t in range(n_new):
            jc = (BAND + t - m) if dil == 1 else (BAND - m)
            slope = jnp.concatenate([slopes[g], jnp.zeros((SUBLANES - HEADS,), F32)])[:, None, None]
            pen = -(slope * (dil * jc).astype(F32))
            ok = (h_key == h_q) & (jc <= BAND)
            tab = jnp.where(ok, pen, -jnp.inf)
            tab = jnp.where(h_q >= HEADS, 0.0, tab)
            per_t.append(tab.reshape(SUBLANES, BAND * HEADS))
        tables.append(jnp.stack(per_t))
    return jnp.stack(tables)


def _sample_attention(slopes, q, caches, kvn):
    nb, n_new = q.shape[0], q.shape[1]
    c_args, c_specs = [], []
    for (win, dil), c in zip(GROUPS, caches):
        res = min(dil, n_new)
        c_args.append(c.reshape(nb, win // dil, dil, 2, HEADS, HEAD_DIM))
        c_specs.append(pl.BlockSpec((1, win // dil, res, 2, HEADS, HEAD_DIM), lambda b: (b, 0, 0, 0, 0, 0)))
    slope_rows = jnp.broadcast_to(slopes[:, :, None], (N_GROUPS, HEADS, HEAD_DIM))
    tables = _sample_tables(slopes, n_new)
    return pl.pallas_call(
        functools.partial(_sample_attn_kernel, n_new=n_new),
        grid=(nb,),
        in_specs=[pl.BlockSpec(slope_rows.shape, lambda b: (0, 0, 0)),
                  pl.BlockSpec(tables.shape, lambda b: (0, 0, 0, 0)),
                  pl.BlockSpec((1, n_new, N_GROUPS, HEADS, HEAD_DIM), lambda b: (b, 0, 0, 0, 0))]
        + c_specs
        + [pl.BlockSpec((N_GROUPS, 1, n_new, 2, HEADS, HEAD_DIM), lambda b: (0, b, 0, 0, 0, 0))],
        out_specs=pl.BlockSpec((1, n_new, HEADS, HEAD_DIM), lambda b: (b, 0, 0, 0)),
        out_shape=jax.ShapeDtypeStruct((nb, n_new, HEADS, HEAD_DIM), F32),
        compiler_params=_cparams(1),
        name="sample_attn",
    )(slope_rows, tables, q, *c_args, kvn)


def _alibi_slopes():
    h = jnp.arange(1, N_GROUPS * HEADS + 1, dtype=F32)
    return jnp.exp2(-8.0 * h / (N_GROUPS * HEADS)).reshape(N_GROUPS, HEADS)


def _prompt_bias(slopes):
    qi = jnp.arange(BAND)[:, None]
    kj = jnp.arange(2 * BAND)[None, :]
    du = BAND + qi - kj
    valid = (du >= 0) & (du <= BAND)
    tables = []
    for g, (_, dil) in enumerate(GROUPS):
        pen = slopes[g][:, None, None] * (dil * du).astype(F32)[None]
        tables.append(jnp.where(valid[None], -pen, -jnp.inf))
    return jnp.stack(tables)


def _channel_layers(ya, yb, sg, x2d, ffn_state, weights, *, tiles, shift, tps):
    w_a, w_b, w_o, w_upb, w_dnb, w_conv_ffn, ln1, ln2 = weights
    (tm_mix, c_mix), (tm_up, c_up), (tm_down, c_down) = tiles
    x1, x1b = _mix(ya, yb, sg, x2d, w_a, w_b, w_o, *ln1, tm=tm_mix, chunk=c_mix)
    hm, cf = _ffn_up(x1b, w_upb, w_conv_ffn, ffn_state, tm=tm_up, chunk=c_up, shift=shift, tps=tps)
    y = _ffn_down(hm, x1, w_dnb, *ln2, tm=tm_down, chunk=c_down)
    return y, cf


def kernel(x_prompt, x_sample, state_conv_mix, cache_kv0, cache_kv1, cache_kv2, state_conv_ffn,
           w_in, w_conv_mix, w_branch_a, w_branch_b, w_out, ln1_g, ln1_b,
           w_up, w_conv_ffn, w_down, ln2_g, ln2_b):
    nb_p, seq, _ = x_prompt.shape
    nb_s, n_new, _ = x_sample.shape
    caches = (cache_kv0, cache_kv1, cache_kv2)

    n_proj = _BLK_GATE * COL_TILE
    w_inb = w_in[:, :n_proj].astype(BF16)
    w_gate = w_in[:, n_proj:].astype(BF16)
    weights = (w_branch_a.astype(BF16), w_branch_b.astype(BF16), w_out.astype(BF16),
               w_up.astype(BF16), w_down.astype(BF16), w_conv_ffn,
               (ln1_g.reshape(1, -1), ln1_b.reshape(1, -1)), (ln2_g.reshape(1, -1), ln2_b.reshape(1, -1)))
    slopes = _alibi_slopes()

    ms = nb_s * n_new
    to_tm = lambda a: jnp.swapaxes(a, 0, 1).reshape(a.shape[0] * a.shape[1], a.shape[2])
    from_tm = lambda a, t: jnp.swapaxes(a.reshape(t, nb_s, a.shape[-1]), 0, 1)
    xs = to_tm(x_sample)
    xbs, yas, cm_s = _proj_mix(xs, w_inb, w_conv_mix, to_tm(state_conv_mix),
                               tm=ms, chunk=ms, shift=nb_s, tps=1)
    q_s, kvn = _sample_qkv(xbs, w_inb, nb=nb_s, n_new=n_new)
    sgs, _ = _proj_gate(xbs, w_gate, tm=ms, tn=1024, chunk=ms)

    tm = 1024
    tps = seq // tm
    x2d = x_prompt.reshape(nb_p * seq, D_MODEL)
    xb, ya, cm_p = _proj_mix(x2d, w_inb, w_conv_mix, None, tm=tm, chunk=PROJ_CHUNK, shift=1, tps=tps)
    qkv, kv_p = [], []
    for g in range(N_GROUPS):
        q, k, v, kv = _proj_qkv(xb, w_inb, g, nseq=nb_p, seqlen=seq, tm=tm, chunk=PROJ_CHUNK)
        qkv += [q, k, v]
        kv_p.append(kv)
    sg, kv_s = _proj_gate(xb, w_gate, tm=seq, tn=1024, chunk=PROJ_CHUNK, caches=caches, kvn=kvn)
    yb = _attention(qkv, _prompt_bias(slopes), nseq=nb_p, seqlen=seq)
    y2d, cf_p = _channel_layers(ya, yb, sg, x2d, None, weights,
                                tiles=((512, 128), (seq, seq // 2), (512, 128)), shift=1, tps=1)
    y_prompt = y2d.reshape(nb_p, seq, D_MODEL)
    cm_p = cm_p[tps - 1::tps]

    yb_s = _sample_attention(slopes, q_s, caches, kvn)
    ybs = to_tm(yb_s.reshape(nb_s, n_new, GROUP_W)).astype(BF16)
    ys, cf_s = _channel_layers(yas, ybs, sgs, xs, to_tm(state_conv_ffn), weights,
                               tiles=((ms, ms),) * 3, shift=nb_s, tps=1)
    y_sample = from_tm(ys, n_new)

    return (y_prompt, y_sample, cm_p, kv_p[0], kv_p[1], kv_p[2], cf_p,
            from_tm(cm_s[0], 2), kv_s[0], kv_s[1], kv_s[2], from_tm(cf_s[0], 2))
```

```python
import functools

import jax
import jax.numpy as jnp
from jax import lax
from jax.experimental import pallas as pl
from jax.experimental.pallas import tpu as pltpu

F32 = jnp.float32
BF16 = jnp.bfloat16

D_MODEL = 2048
HEAD_DIM = 128
HEADS = 4
GROUPS = ((128, 1), (512, 4), (2048, 16))
N_GROUPS = 3
GROUP_W = HEADS * HEAD_DIM
KV_ROWS = 2 * HEADS
BAND = 128
ATT_SCALE = HEAD_DIM ** -0.5
D_FF = 5632
LN_EPS = 1e-5
ALPHA = 2.0 ** 0.25

SUBLANES = 8
LANES = 128
COL_TILE = 512
DEINT_STEP = 4
PROJ_CHUNK = 256
V7X_VMEM_LIMIT = 58 * 1024 * 1024

_BLK_H, _BLK_BG, _BLK_CG = 0, 4, 8
_BLK_Q, _BLK_K, _BLK_V = 12, 15, 18
_BLK_GATE = 21


def _cparams(n_axes, vmem=V7X_VMEM_LIMIT):
    return pltpu.CompilerParams(dimension_semantics=("arbitrary",) * n_axes, vmem_limit_bytes=vmem)


def _dot(a, b):
    return jnp.dot(a, b, preferred_element_type=F32)


def _sigmoid(z):
    return 1.0 / (1.0 + jnp.exp(-z))


def _layernorm(z, g, b):
    mu = jnp.mean(z, axis=-1, keepdims=True)
    zc = z - mu
    var = jnp.mean(zc * zc, axis=-1, keepdims=True)
    return zc * lax.rsqrt(var + LN_EPS) * g + b


def _conv_base(shift):
    return -(-2 * shift // SUBLANES) * SUBLANES


def _load_prev(buf, prev_ref, carry_ref, i, tps, shift):
    p = 2 * shift
    base = _conv_base(shift)
    if prev_ref is not None:
        buf[base - p:base, :] = prev_ref[...]
    else:
        @pl.when(i % tps == 0)
        def _():
            buf[base - p:base, :] = jnp.zeros((p, buf.shape[1]), F32)

        @pl.when(i % tps != 0)
        def _():
            buf[base - p:base, :] = carry_ref[...]


def _conv_chunk(buf, u, w, r0, rows, shift):
    base = _conv_base(shift)
    buf[base + r0:base + r0 + rows, :] = u
    y = w[0:1, :] * buf[base - 2 * shift + r0:base - 2 * shift + r0 + rows, :]
    y = y + w[1:2, :] * buf[base - shift + r0:base - shift + r0 + rows, :]
    return y + w[2:3, :] * u


def _proj_mix_kernel(*refs, tm, chunk, shift, tps, has_prev):
    if has_prev:
        x_ref, wh_ref, wb_ref, wc_ref, wconv_ref, prev_ref, xb_ref, ya_ref, cm_ref, ubuf = refs
        carry_ref = None
    else:
        x_ref, wh_ref, wb_ref, wc_ref, wconv_ref, xb_ref, ya_ref, cm_ref, ubuf, carry_ref = refs
        prev_ref = None
    i, j = pl.program_id(0), pl.program_id(1)

    @pl.when(j == 0)
    def _():
        xb_ref[...] = x_ref[...].astype(BF16)

    _load_prev(ubuf, prev_ref, None if carry_ref is None else carry_ref.at[j], i, tps, shift)
    w = wconv_ref[...]
    for r0 in range(0, tm, chunk):
        xb = xb_ref[r0:r0 + chunk, :]
        u = _dot(xb, wc_ref[...]) * _dot(xb, wh_ref[...])
        y = _conv_chunk(ubuf, u, w, r0, chunk, shift)
        ya_ref[r0:r0 + chunk, :] = (_dot(xb, wb_ref[...]) * y).astype(BF16)
    base = _conv_base(shift)
    last = ubuf[base + tm - 2 * shift:base + tm, :]
    cm_ref[0] = last
    if carry_ref is not None:
        carry_ref[j] = last


def _proj_mix(x2d, w_inb, w_conv, prev, *, tm, chunk, shift, tps):
    m = x2d.shape[0]
    c = w_conv.shape[1]
    tn = COL_TILE
    nj = c // tn
    has_prev = prev is not None
    p = 2 * shift
    wspec = lambda blk: pl.BlockSpec((D_MODEL, tn), lambda i, j: (0, j + blk))
    in_specs = [pl.BlockSpec((tm, D_MODEL), lambda i, j: (i, 0)),
                wspec(_BLK_H), wspec(_BLK_BG), wspec(_BLK_CG),
                pl.BlockSpec((3, tn), lambda i, j: (0, j))]
    args = [x2d, w_inb, w_inb, w_inb, w_conv]
    scratch = [pltpu.VMEM((_conv_base(shift) + tm, tn), F32)]
    if has_prev:
        in_specs.append(pl.BlockSpec((p, tn), lambda i, j: (0, j)))
        args.append(prev)
    else:
        scratch.append(pltpu.VMEM((nj, p, tn), F32))
    return pl.pallas_call(
        functools.partial(_proj_mix_kernel, tm=tm, chunk=chunk, shift=shift, tps=tps, has_prev=has_prev),
        grid=(m // tm, nj),
        in_specs=in_specs,
        out_specs=[
            pl.BlockSpec((tm, D_MODEL), lambda i, j: (i, 0)),
            pl.BlockSpec((tm, tn), lambda i, j: (i, j)),
            pl.BlockSpec((1, p, tn), lambda i, j: (i, 0, j)),
        ],
        out_shape=[
            jax.ShapeDtypeStruct((m, D_MODEL), BF16),
            jax.ShapeDtypeStruct((m, c), BF16),
            jax.ShapeDtypeStruct((m // tm, p, c), F32),
        ],
        scratch_shapes=scratch,
        compiler_params=_cparams(2),
        name="proj_mix",
    )(*args)


def _deinterleave(slab, slab2, idx, chunk, dil):
    sub = chunk // dil
    if dil == DEINT_STEP:
        for rr in range(dil):
            yield rr, slab[idx, pl.ds(rr, sub, stride=dil), :]
        return
    quarter = chunk // DEINT_STEP
    for lo in range(DEINT_STEP):
        slab2[idx, lo * quarter:(lo + 1) * quarter, :] = slab[idx, pl.ds(lo, quarter, stride=DEINT_STEP), :]
    for lo in range(DEINT_STEP):
        for hi in range(dil // DEINT_STEP):
            yield hi * DEINT_STEP + lo, slab2[idx, pl.ds(lo * quarter + hi, sub, stride=DEINT_STEP), :]


def _proj_qkv_kernel(xb_ref, wq_ref, wk_ref, wv_ref, q_ref, k_ref, v_ref, kv_ref, slab, slab2,
                     *, tm, chunk, dil, keep):
    outs = (q_ref, k_ref, v_ref)
    sub = chunk // dil
    for r0 in range(0, tm, chunk):
        xb = xb_ref[r0:r0 + chunk, :]
        res = [_dot(xb, w[...]) for w in (wq_ref, wk_ref, wv_ref)]
        if dil == 1:
            for o_ref, r in zip(outs, res):
                o_ref[0, 0, r0:r0 + chunk, :] = r.astype(BF16)
        else:
            for a, o_ref in enumerate(outs):
                for s in range(HEADS):
                    idx = a * HEADS + s
                    slab[idx] = res[a][:, s * LANES:(s + 1) * LANES]
                    for rr, rows in _deinterleave(slab, slab2, idx, chunk, dil):
                        o_ref[0, rr, r0 // dil:r0 // dil + sub, s * LANES:(s + 1) * LANES] = rows.astype(BF16)
        lo = max(r0, tm - keep)
        if lo < r0 + chunk:
            n = r0 + chunk - lo
            for a in (1, 2):
                for h in range(HEADS):
                    start = (lo - (tm - keep)) * KV_ROWS + (a - 1) * HEADS + h
                    kv_ref[pl.ds(start, n, stride=KV_ROWS), :] = (
                        res[a][lo - r0:chunk, h * LANES:(h + 1) * LANES])


def _proj_qkv(xb, w_inb, g, *, nseq, seqlen, tm, chunk):
    win, dil = GROUPS[g]
    keep_seq = min(win, seqlen)
    keep = min(keep_seq, tm)
    tps = seqlen // tm
    n_u = seqlen // dil
    wspec = lambda blk: pl.BlockSpec((D_MODEL, COL_TILE), lambda i: (0, blk + g))
    ospec = pl.BlockSpec((1, dil, tm // dil, GROUP_W), lambda i: (i // tps, 0, i % tps, 0))
    oshape = jax.ShapeDtypeStruct((nseq, dil, n_u, GROUP_W), BF16)
    kv_index = (lambda i: (i, 0)) if keep_seq == seqlen else (lambda i: (i // tps, 0))
    q, k, v, kv = pl.pallas_call(
        functools.partial(_proj_qkv_kernel, tm=tm, chunk=chunk, dil=dil, keep=keep),
        grid=(nseq * tps,),
        in_specs=[pl.BlockSpec((tm, D_MODEL), lambda i: (i, 0)),
                  wspec(_BLK_Q), wspec(_BLK_K), wspec(_BLK_V)],
        out_specs=[ospec, ospec, ospec, pl.BlockSpec((keep * KV_ROWS, LANES), kv_index)],
        out_shape=[oshape, oshape, oshape,
                   jax.ShapeDtypeStruct((nseq * keep_seq * KV_ROWS, LANES), F32)],
        scratch_shapes=[pltpu.VMEM((3 * HEADS, chunk, LANES), F32)] * 2,
        compiler_params=_cparams(1),
        name=f"proj_qkv_dil{dil}",
    )(xb, w_inb, w_inb, w_inb)
    to_seq = lambda a: a.reshape(nseq, seqlen, GROUP_W)
    return to_seq(q), to_seq(k), to_seq(v), kv.reshape(nseq, keep_seq, 2, HEADS, HEAD_DIM)


def _proj_gate_kernel(*refs, tm, chunk, n_caches, halves, drop):
    xb_ref, w_ref = refs[0:2]
    cache_refs = refs[2:2 + 2 * n_caches]
    new_ref = refs[2 + 2 * n_caches] if n_caches else None
    o_ref = refs[2 + 2 * n_caches + (1 if n_caches else 0)]
    shifted_refs = refs[len(refs) - n_caches:] if n_caches else ()
    for r0 in range(0, tm, chunk):
        z = _dot(xb_ref[r0:r0 + chunk, :], w_ref[...])
        o_ref[r0:r0 + chunk, :] = _sigmoid(z).astype(BF16)
    if n_caches:
        step = pl.program_id(0) * pl.num_programs(1) + pl.program_id(1)
        is_last = step % halves == halves - 1
        for g in range(n_caches):
            c_ref, nxt_ref, s_ref = cache_refs[2 * g], cache_refs[2 * g + 1], shifted_refs[g]
            rows = c_ref.shape[1]
            s_ref[0, 0:rows - drop, :] = c_ref[0, drop:rows, :]

            @pl.when(jnp.logical_not(is_last))
            def _(nxt_ref=nxt_ref, s_ref=s_ref, rows=rows):
                s_ref[0, rows - drop:rows, :] = nxt_ref[0]

            @pl.when(is_last)
            def _(s_ref=s_ref, rows=rows, g=g):
                s_ref[0, rows - drop:rows, :] = new_ref[g, 0]


def _proj_gate(xb, w_gate, *, tm, tn, chunk, caches=(), kvn=None):
    m = xb.shape[0]
    n = w_gate.shape[1]
    n_i = m // tm
    steps = (n // tn) * n_i
    in_specs = [pl.BlockSpec((tm, D_MODEL), lambda c, i: (i, 0)),
                pl.BlockSpec((D_MODEL, tn), lambda c, i: (0, c))]
    out_specs = [pl.BlockSpec((tm, tn), lambda c, i: (i, c))]
    out_shape = [jax.ShapeDtypeStruct((m, n), BF16)]
    args = [xb, w_gate]
    halves, drop = 1, 0
    if caches:
        nb, n_new = kvn.shape[1], kvn.shape[2]
        halves = steps // nb
        drop = n_new * KV_ROWS
        blk = lambda c, i: ((c * n_i + i) // halves, (c * n_i + i) % halves)
        for cache in caches:
            rows = cache.shape[1] * KV_ROWS // halves
            flat = cache.reshape(nb, cache.shape[1] * KV_ROWS, LANES)
            per = rows // drop
            in_specs += [
                pl.BlockSpec((1, rows, LANES), lambda c, i: (*blk(c, i), 0)),
                pl.BlockSpec((1, drop, LANES), lambda c, i, per=per: (
                    blk(c, i)[0], jnp.minimum(blk(c, i)[1] + 1, halves - 1) * per, 0)),
            ]
            args += [flat, flat]
            out_specs.append(pl.BlockSpec((1, rows, LANES), lambda c, i: (*blk(c, i), 0)))
            out_shape.append(jax.ShapeDtypeStruct(flat.shape, cache.dtype))
        in_specs.append(pl.BlockSpec((len(caches), 1, drop, LANES), lambda c, i: (0, blk(c, i)[0], 0, 0)))
        args.append(kvn.reshape(len(caches), nb, drop, LANES))
    outs = pl.pallas_call(
        functools.partial(_proj_gate_kernel, tm=tm, chunk=chunk, n_caches=len(caches), halves=halves, drop=drop),
        grid=(n // tn, n_i),
        in_specs=in_specs,
        out_specs=out_specs,
        out_shape=out_shape,
        compiler_params=_cparams(2),
        name="proj_gate",
    )(*args)
    return outs[0], [o.reshape(c.shape) for o, c in zip(outs[1:], caches)]


ATTN_GROUP_ORDER = (2, 1, 0)
ATTN_UNROLL = 8


def _attn_block(qkv_refs, bias_ref, yb_ref, state, g, row0, pos0, first):
    q_ref, k_ref, v_ref = qkv_refs
    acc, m_s, l_s = state
    dil = GROUPS[g][1]
    opens, closes = g == ATTN_GROUP_ORDER[0], g == ATTN_GROUP_ORDER[-1]
    rows = pl.ds(pl.multiple_of(pos0, BAND), BAND) if dil == 1 else pl.ds(pos0, BAND, stride=dil)
    qrows = pl.ds(row0, BAND)
    krows = qrows if first else pl.ds(row0 - BAND, 2 * BAND)
    heads = lambda ref, r: jnp.stack([ref[0, r, h * HEAD_DIM:(h + 1) * HEAD_DIM] for h in range(HEADS)])
    q, kw, vw = heads(q_ref, qrows), heads(k_ref, krows), heads(v_ref, krows)
    bias = bias_ref[g, :, :, BAND:] if first else bias_ref[g]
    s = jnp.einsum("hqd,hkd->hqk", q, kw, preferred_element_type=F32) * ATT_SCALE + bias
    mx = jnp.max(s, axis=-1, keepdims=True)
    p = jnp.exp(s - mx)
    num = jnp.einsum("hqk,hkd->hqd", p.astype(BF16), vw, preferred_element_type=F32)
    wide = (HEADS, BAND, HEAD_DIM)
    mx = jnp.broadcast_to(mx, wide)
    den = jnp.broadcast_to(jnp.sum(p, axis=-1, keepdims=True), wide)
    if not opens:
        m_old = m_s[:, rows, :]
        m_new = jnp.maximum(m_old, mx)
        w_old, w_new = jnp.exp(m_old - m_new), jnp.exp(mx - m_new)
        num = acc[:, rows, :] * w_old + num * w_new
        den = l_s[:, rows, :] * w_old + den * w_new
        mx = m_new
    if closes:
        out = (num / den).astype(BF16)
        for h in range(HEADS):
            yb_ref[rows, h * HEAD_DIM:(h + 1) * HEAD_DIM] = out[h]
    else:
        acc[:, rows, :] = num
        m_s[:, rows, :] = mx
        l_s[:, rows, :] = den


def _attn_kernel(*refs, seqlen):
    qkv = refs[0:9]
    bias_ref, yb_ref = refs[9:11]
    state = refs[11:14]
    for g in ATTN_GROUP_ORDER:
        dil = GROUPS[g][1]
        n_u = seqlen // dil
        nblk = n_u // BAND
        block = functools.partial(_attn_block, qkv[3 * g:3 * g + 3], bias_ref, yb_ref, state, g)

        def per_residue(r, carry, block=block, n_u=n_u, nblk=nblk, dil=dil):
            base = pl.multiple_of(r * n_u, BAND)
            block(base, r, True)

            def later(qb, c):
                block(pl.multiple_of(base + qb * BAND, BAND), qb * (BAND * dil) + r, False)
                return c

            if nblk <= ATTN_UNROLL:
                for qb in range(1, nblk):
                    later(qb, carry)
                return carry
            return lax.fori_loop(1, nblk, later, carry, unroll=ATTN_UNROLL // 2)

        lax.fori_loop(0, dil, per_residue, 0, unroll=ATTN_UNROLL if nblk == 1 else 1)


def _attention(qkv, bias, *, nseq, seqlen):
    spec = pl.BlockSpec((1, seqlen, GROUP_W), lambda b: (b, 0, 0))
    return pl.pallas_call(
        functools.partial(_attn_kernel, seqlen=seqlen),
        grid=(nseq,),
        in_specs=[spec] * 9 + [pl.BlockSpec(bias.shape, lambda b: (0, 0, 0, 0))],
        out_specs=pl.BlockSpec((seqlen, GROUP_W), lambda b: (b, 0)),
        out_shape=jax.ShapeDtypeStruct((nseq * seqlen, GROUP_W), BF16),
        scratch_shapes=[pltpu.VMEM((HEADS, seqlen, HEAD_DIM), F32)] * 3,
        compiler_params=_cparams(1),
        name="attention",
    )(*qkv, bias)


def _mix_kernel(ya_ref, yb_ref, ga_ref, gb_ref, x_ref, wa_ref, wb_ref, wo_ref, g_ref, b_ref,
                x1_ref, x1b_ref, *, tm, chunk):
    for r0 in range(0, tm, chunk):
        rows = slice(r0, r0 + chunk)
        a = _dot(ya_ref[rows, :], wa_ref[...])
        b = _dot(yb_ref[rows, :], wb_ref[...])
        mixed = ga_ref[rows, :].astype(F32) * a + gb_ref[rows, :].astype(F32) * b
        z = ALPHA * x_ref[rows, :] + _dot(mixed.astype(BF16), wo_ref[...])
        out = _layernorm(z, g_ref[...], b_ref[...])
        x1_ref[rows, :] = out
        x1b_ref[rows, :] = out.astype(BF16)


def _mix(ya, yb, sg, x2d, w_a, w_b, w_o, ln_g, ln_b, *, tm, chunk):
    m = x2d.shape[0]
    row = lambda w, c=0: pl.BlockSpec((tm, w), lambda i: (i, c))
    const = lambda shape: pl.BlockSpec(shape, lambda i: (0, 0), pipeline_mode=pl.Buffered(1))
    return pl.pallas_call(
        functools.partial(_mix_kernel, tm=tm, chunk=chunk),
        grid=(m // tm,),
        in_specs=[row(D_MODEL), row(GROUP_W), row(D_MODEL, 0), row(D_MODEL, 1), row(D_MODEL),
                  const((D_MODEL, D_MODEL)), const((GROUP_W, D_MODEL)), const((D_MODEL, D_MODEL)),
                  const((1, D_MODEL)), const((1, D_MODEL))],
        out_specs=[row(D_MODEL), row(D_MODEL)],
        out_shape=[jax.ShapeDtypeStruct((m, D_MODEL), F32), jax.ShapeDtypeStruct((m, D_MODEL), BF16)],
        compiler_params=_cparams(1),
        name="mix_out",
    )(ya, yb, sg, sg, x2d, w_a, w_b, w_o, ln_g, ln_b)


def _ffn_up_kernel(*refs, tm, chunk, shift, tps, has_prev):
    if has_prev:
        x1b_ref, wa_ref, wb_ref, wconv_ref, prev_ref, hm_ref, cf_ref, abuf, wa_s, wb_s = refs
        carry_ref = None
    else:
        x1b_ref, wa_ref, wb_ref, wconv_ref, hm_ref, cf_ref, abuf, wa_s, wb_s, carry_ref = refs
        prev_ref = None
    i = pl.program_id(1)

    @pl.when(i == 0)
    def _():
        wa_s[...] = wa_ref[...].astype(BF16)
        wb_s[...] = wb_ref[...].astype(BF16)

    _load_prev(abuf, prev_ref, carry_ref, i, tps, shift)
    w = wconv_ref[...]
    for r0 in range(0, tm, chunk):
        xb = x1b_ref[r0:r0 + chunk, :]
        ac = _conv_chunk(abuf, _dot(xb, wa_s[...]), w, r0, chunk, shift)
        hm_ref[r0:r0 + chunk, :] = (ac * _sigmoid(ac) * _dot(xb, wb_s[...])).astype(BF16)
    base = _conv_base(shift)
    last = abuf[base + tm - 2 * shift:base + tm, :]
    cf_ref[0] = last
    if carry_ref is not None:
        carry_ref[...] = last


def _ffn_up(x1b, w_up, w_conv, prev, *, tm, chunk, shift, tps):
    m = x1b.shape[0]
    tf = COL_TILE
    nf = D_FF // tf
    has_prev = prev is not None
    p = 2 * shift
    in_specs = [
        pl.BlockSpec((tm, D_MODEL), lambda f, i: (i, 0)),
        pl.BlockSpec((D_MODEL, tf), lambda f, i: (0, f)),
        pl.BlockSpec((D_MODEL, tf), lambda f, i: (0, f + nf)),
        pl.BlockSpec((3, tf), lambda f, i: (0, f)),
    ]
    args = [x1b, w_up, w_up, w_conv]
    scratch = [pltpu.VMEM((_conv_base(shift) + tm, tf), F32),
               pltpu.VMEM((D_MODEL, tf), BF16), pltpu.VMEM((D_MODEL, tf), BF16)]
    if has_prev:
        in_specs.append(pl.BlockSpec((p, tf), lambda f, i: (0, f)))
        args.append(prev)
    else:
        scratch.append(pltpu.VMEM((p, tf), F32))
    return pl.pallas_call(
        functools.partial(_ffn_up_kernel, tm=tm, chunk=chunk, shift=shift, tps=tps, has_prev=has_prev),
        grid=(nf, m // tm),
        in_specs=in_specs,
        out_specs=[pl.BlockSpec((tm, tf), lambda f, i: (i, f)),
                   pl.BlockSpec((1, p, tf), lambda f, i: (i, 0, f))],
        out_shape=[jax.ShapeDtypeStruct((m, D_FF), BF16),
                   jax.ShapeDtypeStruct((m // tm, p, D_FF), F32)],
        scratch_shapes=scratch,
        compiler_params=_cparams(2),
        name="ffn_up",
    )(*args)


def _ffn_down_kernel(hm_ref, x1_ref, wd_ref, g_ref, b_ref, y_ref, *, tm, chunk):
    for r0 in range(0, tm, chunk):
        rows = slice(r0, r0 + chunk)
        z = ALPHA * x1_ref[rows, :] + _dot(hm_ref[rows, :], wd_ref[...])
        y_ref[rows, :] = _layernorm(z, g_ref[...], b_ref[...])


def _ffn_down(hm, x1, w_dnb, ln_g, ln_b, *, tm, chunk):
    m = x1.shape[0]
    const = lambda shape: pl.BlockSpec(shape, lambda i: (0, 0), pipeline_mode=pl.Buffered(1))
    return pl.pallas_call(
        functools.partial(_ffn_down_kernel, tm=tm, chunk=chunk),
        grid=(m // tm,),
        in_specs=[pl.BlockSpec((tm, D_FF), lambda i: (i, 0)),
                  pl.BlockSpec((tm, D_MODEL), lambda i: (i, 0)),
                  const((D_FF, D_MODEL)), const((1, D_MODEL)), const((1, D_MODEL))],
        out_specs=pl.BlockSpec((tm, D_MODEL), lambda i: (i, 0)),
        out_shape=jax.ShapeDtypeStruct((m, D_MODEL), F32),
        compiler_params=_cparams(1),
        name="ffn_down",
    )(hm, x1, w_dnb, ln_g, ln_b)


def _sample_qkv_kernel(xb_ref, wq_ref, wk_ref, wv_ref, q_ref, kvn_ref, *, nb, n_new):
    g = pl.program_id(0)
    xb = xb_ref[...]
    res = [_dot(xb, w[...]) for w in (wq_ref, wk_ref, wv_ref)]
    q_stride = n_new * N_GROUPS * HEADS
    kv_stride = n_new * KV_ROWS
    for t in range(n_new):
        for h in range(HEADS):
            piece = lambda r: r[t * nb:(t + 1) * nb, h * LANES:(h + 1) * LANES]
            q_ref[pl.ds(t * N_GROUPS * HEADS + g * HEADS + h, nb, stride=q_stride), :] = piece(res[0])
            for a in (1, 2):
                start = g * nb * kv_stride + t * KV_ROWS + (a - 1) * HEADS + h
                kvn_ref[pl.ds(start, nb, stride=kv_stride), :] = piece(res[a])


def _sample_qkv(xb, w_inb, *, nb, n_new):
    ms = nb * n_new
    wspec = lambda blk: pl.BlockSpec((D_MODEL, COL_TILE), lambda g: (0, blk + g))
    q_rows = ms * N_GROUPS * HEADS
    kv_rows = N_GROUPS * ms * KV_ROWS
    q, kvn = pl.pallas_call(
        functools.partial(_sample_qkv_kernel, nb=nb, n_new=n_new),
        grid=(N_GROUPS,),
        in_specs=[pl.BlockSpec((ms, D_MODEL), lambda g: (0, 0)),
                  wspec(_BLK_Q), wspec(_BLK_K), wspec(_BLK_V)],
        out_specs=[pl.BlockSpec((q_rows, LANES), lambda g: (0, 0)),
                   pl.BlockSpec((kv_rows, LANES), lambda g: (0, 0))],
        out_shape=[jax.ShapeDtypeStruct((q_rows, LANES), F32),
                   jax.ShapeDtypeStruct((kv_rows, LANES), F32)],
        compiler_params=_cparams(1),
        name="sample_qkv",
    )(xb, w_inb, w_inb, w_inb)
    return (q.reshape(nb, n_new, N_GROUPS, HEADS, HEAD_DIM),
            kvn.reshape(N_GROUPS, nb, n_new, 2, HEADS, HEAD_DIM))


def _sample_attn_kernel(slope_ref, table_ref, q_ref, c0_ref, c1_ref, c2_ref, kvn_ref, o_ref, *, n_new):
    c_refs = (c0_ref, c1_ref, c2_ref)
    n_idx = lax.broadcasted_iota(jnp.int32, (n_new, 1, 1), 0)
    pad = jnp.zeros((SUBLANES - HEADS, HEAD_DIM), F32)
    for t in range(n_new):
        s_cache, s_new, v_cache, v_new = [], [], [], []
        for g, (_, dil) in enumerate(GROUPS):
            q = q_ref[0, t, g]
            res = 0 if dil == 1 else t
            kc = c_refs[g][0, :, res, 0].reshape(BAND * HEADS, HEAD_DIM)
            v_cache.append(c_refs[g][0, :, res, 1].reshape(BAND * HEADS, HEAD_DIM))
            sc = lax.dot_general(jnp.concatenate([q, pad], axis=0), kc, (((1,), (1,)), ((), ())),
                                 preferred_element_type=F32)
            s_cache.append(sc * ATT_SCALE + table_ref[g, t])
            if dil == 1:
                slope = slope_ref[g][None, :, 0:1]
                kn, vn = kvn_ref[g, 0, :, 0], kvn_ref[g, 0, :, 1]
                jn = t - n_idx
                sn = jnp.sum(kn * q[None], axis=-1, keepdims=True) * ATT_SCALE - slope * jn.astype(F32)
                sn = jnp.where(jn >= 0, sn, -jnp.inf)
            else:
                kn, vn = kvn_ref[g, 0, t:t + 1, 0], kvn_ref[g, 0, t:t + 1, 1]
                sn = jnp.sum(kn * q[None], axis=-1, keepdims=True) * ATT_SCALE
            s_new.append(sn)
            v_new.append(vn)
        mx = functools.reduce(jnp.maximum, [jnp.max(s, axis=-1, keepdims=True) for s in s_cache])
        mx_h = functools.reduce(jnp.maximum, [jnp.max(s, axis=0) for s in s_new] + [mx[0:HEADS]])
        mx = jnp.concatenate([mx_h, mx[HEADS:]], axis=0)
        p_cache = [jnp.exp(s - mx) for s in s_cache]
        p_new = [jnp.exp(s - mx_h[None]) for s in s_new]
        den = functools.reduce(jnp.add, [jnp.sum(p, axis=-1, keepdims=True)[0:HEADS] for p in p_cache]
                               + [jnp.sum(p, axis=0) for p in p_new])
        acc = functools.reduce(jnp.add, [_dot(p, v)[0:HEADS] for p, v in zip(p_cache, v_cache)]
                               + [jnp.sum(p * v, axis=0) for p, v in zip(p_new, v_new)])
        o_ref[0, t] = acc / den


def _sample_tables(slopes, n_new):
    m = jnp.arange(BAND)[None, :, None]
    h_key = jnp.arange(HEADS)[None, None, :]
    h_q = jnp.arange(SUBLANES)[:, None, None]
    tables = []
    for g, (_, dil) in enumerate(GROUPS):
        per_t = []
        for t in range(n_new):
            jc = (BAND + t - m) if dil == 1 else (BAND - m)
            slope = jnp.concatenate([slopes[g], jnp.zeros((SUBLANES - HEADS,), F32)])[:, None, None]
            pen = -(slope * (dil * jc).astype(F32))
            ok = (h_key == h_q) & (jc <= BAND)
            tab = jnp.where(ok, pen, -jnp.inf)
            tab = jnp.where(h_q >= HEADS, 0.0, tab)
            per_t.append(tab.reshape(SUBLANES, BAND * HEADS))
        tables.append(jnp.stack(per_t))
    return jnp.stack(tables)


def _sample_attention(slopes, q, caches, kvn):
    nb, n_new = q.shape[0], q.shape[1]
    c_args, c_specs = [], []
    for (win, dil), c in zip(GROUPS, caches):
        res = min(dil, n_new)
        c_args.append(c.reshape(nb, win // dil, dil, 2, HEADS, HEAD_DIM))
        c_specs.append(pl.BlockSpec((1, win // dil, res, 2, HEADS, HEAD_DIM), lambda b: (b, 0, 0, 0, 0, 0)))
    slope_rows = jnp.broadcast_to(slopes[:, :, None], (N_GROUPS, HEADS, HEAD_DIM))
    tables = _sample_tables(slopes, n_new)
    return pl.pallas_call(
        functools.partial(_sample_attn_kernel, n_new=n_new),
        grid=(nb,),
        in_specs=[pl.BlockSpec(slope_rows.shape, lambda b: (0, 0, 0)),
                  pl.BlockSpec(tables.shape, lambda b: (0, 0, 0, 0)),
                  pl.BlockSpec((1, n_new, N_GROUPS, HEADS, HEAD_DIM), lambda b: (b, 0, 0, 0, 0))]
        + c_specs
        + [pl.BlockSpec((N_GROUPS, 1, n_new, 2, HEADS, HEAD_DIM), lambda b: (0, b, 0, 0, 0, 0))],
        out_specs=pl.BlockSpec((1, n_new, HEADS, HEAD_DIM), lambda b: (b, 0, 0, 0)),
        out_shape=jax.ShapeDtypeStruct((nb, n_new, HEADS, HEAD_DIM), F32),
        compiler_params=_cparams(1),
        name="sample_attn",
    )(slope_rows, tables, q, *c_args, kvn)


def _alibi_slopes():
    h = jnp.arange(1, N_GROUPS * HEADS + 1, dtype=F32)
    return jnp.exp2(-8.0 * h / (N_GROUPS * HEADS)).reshape(N_GROUPS, HEADS)


def _prompt_bias(slopes):
    qi = jnp.arange(BAND)[:, None]
    kj = jnp.arange(2 * BAND)[None, :]
    du = BAND + qi - kj
    valid = (du >= 0) & (du <= BAND)
    tables = []
    for g, (_, dil) in enumerate(GROUPS):
        pen = slopes[g][:, None, None] * (dil * du).astype(F32)[None]
        tables.append(jnp.where(valid[None], -pen, -jnp.inf))
    return jnp.stack(tables)


def _channel_layers(ya, yb, sg, x2d, ffn_state, weights, *, tiles, shift, tps):
    w_a, w_b, w_o, w_up, w_dnb, w_conv_ffn, ln1, ln2 = weights
    (tm_mix, c_mix), (tm_up, c_up), (tm_down, c_down) = tiles
    x1, x1b = _mix(ya, yb, sg, x2d, w_a, w_b, w_o, *ln1, tm=tm_mix, chunk=c_mix)
    hm, cf = _ffn_up(x1b, w_up, w_conv_ffn, ffn_state, tm=tm_up, chunk=c_up, shift=shift, tps=tps)
    y = _ffn_down(hm, x1, w_dnb, *ln2, tm=tm_down, chunk=c_down)
    return y, cf


def kernel(x_prompt, x_sample, state_conv_mix, cache_kv0, cache_kv1, cache_kv2, state_conv_ffn,
           w_in, w_conv_mix, w_branch_a, w_branch_b, w_out, ln1_g, ln1_b,
           w_up, w_conv_ffn, w_down, ln2_g, ln2_b):
    nb_p, seq, _ = x_prompt.shape
    nb_s, n_new, _ = x_sample.shape
    caches = (cache_kv0, cache_kv1, cache_kv2)

    w_inb = w_in.astype(BF16)
    w_gate = w_inb[:, _BLK_GATE * COL_TILE:]
    weights = (w_branch_a.astype(BF16), w_branch_b.astype(BF16), w_out.astype(BF16),
               w_up, w_down.astype(BF16), w_conv_ffn,
               (ln1_g.reshape(1, -1), ln1_b.reshape(1, -1)), (ln2_g.reshape(1, -1), ln2_b.reshape(1, -1)))
    slopes = _alibi_slopes()

    ms = nb_s * n_new
    to_tm = lambda a: jnp.swapaxes(a, 0, 1).reshape(a.shape[0] * a.shape[1], a.shape[2])
    from_tm = lambda a, t: jnp.swapaxes(a.reshape(t, nb_s, a.shape[-1]), 0, 1)
    xs = to_tm(x_sample)
    xbs, yas, cm_s = _proj_mix(xs, w_inb, w_conv_mix, to_tm(state_conv_mix),
                               tm=ms, chunk=ms, shift=nb_s, tps=1)
    q_s, kvn = _sample_qkv(xbs, w_inb, nb=nb_s, n_new=n_new)
    sgs, _ = _proj_gate(xbs, w_gate, tm=ms, tn=1024, chunk=ms)

    tm = 1024
    tps = seq // tm
    x2d = x_prompt.reshape(nb_p * seq, D_MODEL)
    xb, ya, cm_p = _proj_mix(x2d, w_inb, w_conv_mix, None, tm=tm, chunk=PROJ_CHUNK, shift=1, tps=tps)
    qkv, kv_p = [], []
    for g in range(N_GROUPS):
        q, k, v, kv = _proj_qkv(xb, w_inb, g, nseq=nb_p, seqlen=seq, tm=tm, chunk=PROJ_CHUNK)
        qkv += [q, k, v]
        kv_p.append(kv)
    sg, kv_s = _proj_gate(xb, w_gate, tm=seq, tn=1024, chunk=PROJ_CHUNK, caches=caches, kvn=kvn)
    yb = _attention(qkv, _prompt_bias(slopes), nseq=nb_p, seqlen=seq)
    y2d, cf_p = _channel_layers(ya, yb, sg, x2d, None, weights,
                                tiles=((512, 128), (seq, seq // 2), (512, 128)), shift=1, tps=1)
    y_prompt = y2d.reshape(nb_p, seq, D_MODEL)
    cm_p = cm_p[tps - 1::tps]

    yb_s = _sample_attention(slopes, q_s, caches, kvn)
    ybs = to_tm(yb_s.reshape(nb_s, n_new, GROUP_W)).astype(BF16)
    ys, cf_s = _channel_layers(yas, ybs, sgs, xs, to_tm(state_conv_ffn), weights,
                               tiles=((ms, ms),) * 3, shift=nb_s, tps=1)
    y_sample = from_tm(ys, n_new)

    return (y_prompt, y_sample, cm_p, kv_p[0], kv_p[1], kv_p[2], cf_p,
            from_tm(cm_s[0], 2), kv_s[0], kv_s[1], kv_s[2], from_tm(cf_s[0], 2))
```
